```python
import math
import jax, jax.numpy as jnp
from jax import lax
import numpy as np

D_MODEL = 1024
BATCH = 2
SEQ = 8192
DEPTH = 1
DEC_BATCH = 32
DEC_SEQ = 4
PAST_LEN = 16384
PAGE_SIZE = 128

D_MIX = D_MODEL
N_ATTN_HEADS = 8
HEAD_DIM = 64
ATTN_WIDTH = N_ATTN_HEADS * HEAD_DIM
CONV_CH = D_MIX - ATTN_WIDTH
CONV_W = 3
D_FF = 2816
D_PROJ = 3 * ATTN_WIDTH + N_ATTN_HEADS + 3 * CONV_CH
Q_BLOCK = 128
EPS = 1e-6
FORGET_BIAS_LO = 2.0
FORGET_BIAS_HI = 10.0

kernel_name = 'hymba_fox_shortconv_macaron_step'


def rms_norm(x, g):
    xf = x.astype(jnp.float32)
    y = xf * lax.rsqrt(jnp.mean(xf * xf, axis=-1, keepdims=True) + EPS)
    return (y * g.astype(jnp.float32)).astype(x.dtype)


def swiglu(x, w1, w3, w2):
    return (jax.nn.silu(x @ w1) * (x @ w3)) @ w2


def combined_projection(h, w_in, b_f):
    p = h @ w_in
    A, H, C = ATTN_WIDTH, N_ATTN_HEADS, CONV_CH
    q, k, v, fl, bg, cg, u = jnp.split(
        p, [A, 2 * A, 3 * A, 3 * A + H, 3 * A + H + C, 3 * A + H + 2 * C], axis=-1)
    Bsz, L = h.shape[0], h.shape[1]
    q = q.reshape(Bsz, L, H, HEAD_DIM)
    k = k.reshape(Bsz, L, H, HEAD_DIM)
    v = v.reshape(Bsz, L, H, HEAD_DIM)
    logf = jax.nn.log_sigmoid((fl + b_f).astype(jnp.float32))
    uc = cg * u
    return q, k, v, logf, bg, uc


def fox_prompt(q, k, v, logf):
    Bsz, L, H, Dh = q.shape
    n_blk = L // Q_BLOCK
    scale = 1.0 / math.sqrt(Dh)
    c = lax.cumsum(logf, axis=1)
    c_keys = jnp.transpose(c, (0, 2, 1))[:, :, None, :]
    kpos = jnp.arange(L)
    qb = q.reshape(Bsz, n_blk, Q_BLOCK, H, Dh).transpose(1, 0, 2, 3, 4)
    cb = c.reshape(Bsz, n_blk, Q_BLOCK, H).transpose(1, 0, 2, 3)

    def one_block(args):
        qi, ci, blk = args
        s = jnp.einsum('bqhd,bkhd->bhqk', qi, k).astype(jnp.float32) * scale
        s = s + jnp.transpose(ci, (0, 2, 1))[..., None] - c_keys
        qpos = blk * Q_BLOCK + jnp.arange(Q_BLOCK)
        mask = kpos[None, :] <= qpos[:, None]
        s = jnp.where(mask[None, None], s, -jnp.inf)
        pr = jax.nn.softmax(s, axis=-1).astype(v.dtype)
        return jnp.einsum('bhqk,bkhd->bqhd', pr, v)

    o = lax.map(one_block, (qb, cb, jnp.arange(n_blk)))
    return o.transpose(1, 0, 2, 3, 4).reshape(Bsz, L, H * Dh)


def fox_sample(q, k, v, logf, k_past, v_past, logf_past):
    Bsz, T, H, Dh = q.shape
    P = k_past.shape[1]
    scale = 1.0 / math.sqrt(Dh)
    lp = logf_past.astype(jnp.float32)
    suffix = lax.cumsum(lp, axis=1, reverse=True) - lp
    c_new = lax.cumsum(logf, axis=1)
    c_q = jnp.transpose(c_new, (0, 2, 1))[..., None]
    s_past = jnp.einsum('bqhd,bkhd->bhqk', q, k_past).astype(jnp.float32) * scale
    s_past = s_past + c_q + jnp.transpose(suffix, (0, 2, 1))[:, :, None, :]
    s_new = jnp.einsum('bqhd,bkhd->bhqk', q, k).astype(jnp.float32) * scale
    s_new = s_new + c_q - jnp.transpose(c_new, (0, 2, 1))[:, :, None, :]
    tpos = jnp.arange(T)
    s_new = jnp.where((tpos[None, :] <= tpos[:, None])[None, None], s_new, -jnp.inf)
    pr = jax.nn.softmax(jnp.concatenate([s_past, s_new], axis=-1), axis=-1).astype(v.dtype)
    o = (jnp.einsum('bhqk,bkhd->bqhd', pr[..., :P], v_past)
         + jnp.einsum('bhqk,bkhd->bqhd', pr[..., P:], v))
    return o.reshape(Bsz, T, H * Dh)


def short_conv(u, buf, conv_w):
    L = u.shape[1]
    ext = jnp.concatenate([buf, u], axis=1)
    y = conv_w[0] * ext[:, 0:L]
    for i in range(1, CONV_W):
        y = y + conv_w[i] * ext[:, i:i + L]
    return y, ext[:, L:]


def merge_groups(o_attn, bg, y_conv, g_ao, g_co, w_out):
    conv_out = bg * y_conv
    return jnp.concatenate([rms_norm(o_attn, g_ao), rms_norm(conv_out, g_co)], axis=-1) @ w_out


def setup_inputs(seed: int = 0) -> dict:
    key = jax.random.key(seed)
    ks = jax.random.split(key, 24)
    n_pages = PAST_LEN // PAGE_SIZE
    n_used = DEC_BATCH * n_pages
    n_pool = n_used + n_used // 4
    f32 = jnp.float32

    def nrm(k, shape, scale):
        return jax.random.normal(k, shape, f32) * scale

    def gain(k, n):
        return 1.0 + 0.02 * jax.random.normal(k, (DEPTH, n), f32)

    head_bias = jnp.linspace(FORGET_BIAS_LO, FORGET_BIAS_HI, N_ATTN_HEADS, dtype=f32)
    page_table = jax.random.permutation(ks[0], n_pool)[:n_used].reshape(DEC_BATCH, n_pages).astype(jnp.int32)
    return {
        'x_prompt': nrm(ks[1], (BATCH, SEQ, D_MODEL), 1.0),
        'x_sample': nrm(ks[2], (DEC_BATCH, DEC_SEQ, D_MODEL), 1.0),
        'cache_k': nrm(ks[3], (DEPTH, n_pool, PAGE_SIZE, N_ATTN_HEADS, HEAD_DIM), 1.0),
        'cache_v': nrm(ks[4], (DEPTH, n_pool, PAGE_SIZE, N_ATTN_HEADS, HEAD_DIM), 1.0),
        'cache_logf': jax.nn.log_sigmoid(head_bias + jax.random.normal(ks[5], (DEPTH, n_pool, PAGE_SIZE, N_ATTN_HEADS), f32)),
        'state_conv': nrm(ks[6], (DEPTH, DEC_BATCH, CONV_W - 1, CONV_CH), 1.0),
        'page_table': page_table,
        'g_ffn1': gain(ks[7], D_MODEL),
        'w1_ffn1': nrm(ks[8], (DEPTH, D_MODEL, D_FF), D_MODEL ** -0.5),
        'w3_ffn1': nrm(ks[9], (DEPTH, D_MODEL, D_FF), D_MODEL ** -0.5),
        'w2_ffn1': nrm(ks[10], (DEPTH, D_FF, D_MODEL), D_FF ** -0.5),
        'g_mix': gain(ks[11], D_MODEL),
        'w_in': nrm(ks[12], (DEPTH, D_MODEL, D_PROJ), D_MODEL ** -0.5),
        'b_f': head_bias + 0.3 * jax.random.normal(ks[13], (DEPTH, N_ATTN_HEADS), f32),
        'conv_w': nrm(ks[14], (DEPTH, CONV_W, CONV_CH), CONV_W ** -0.5),
        'g_attn_out': gain(ks[15], ATTN_WIDTH),
        'g_conv_out': gain(ks[16], CONV_CH),
        'w_out': nrm(ks[17], (DEPTH, D_MIX, D_MODEL), D_MIX ** -0.5),
        'g_ffn2': gain(ks[18], D_MODEL),
        'w1_ffn2': nrm(ks[19], (DEPTH, D_MODEL, D_FF), D_MODEL ** -0.5),
        'w3_ffn2': nrm(ks[20], (DEPTH, D_MODEL, D_FF), D_MODEL ** -0.5),
        'w2_ffn2': nrm(ks[21], (DEPTH, D_FF, D_MODEL), D_FF ** -0.5),
        'g_final': 1.0 + 0.02 * jax.random.normal(ks[22], (D_MODEL,), f32),
    }


def reference(x_prompt, x_sample, cache_k, cache_v, cache_logf, state_conv, page_table,
              g_ffn1, w1_ffn1, w3_ffn1, w2_ffn1, g_mix, w_in, b_f, conv_w,
              g_attn_out, g_conv_out, w_out, g_ffn2, w1_ffn2, w3_ffn2, w2_ffn2, g_final):
    xp, xs = x_prompt, x_sample
    n_past = page_table.shape[1] * PAGE_SIZE
    kp_l, vp_l, fp_l, cp_l, ks_l, vs_l, fs_l, cs_l = [], [], [], [], [], [], [], []
    for l in range(DEPTH):
        xp = xp + 0.5 * swiglu(rms_norm(xp, g_ffn1[l]), w1_ffn1[l], w3_ffn1[l], w2_ffn1[l])
        xs = xs + 0.5 * swiglu(rms_norm(xs, g_ffn1[l]), w1_ffn1[l], w3_ffn1[l], w2_ffn1[l])

        q, k, v, logf, bg, uc = combined_projection(rms_norm(xp, g_mix[l]), w_in[l], b_f[l])
        o_attn = fox_prompt(q, k, v, logf)
        y_conv, buf_p = short_conv(uc, jnp.zeros((uc.shape[0], CONV_W - 1, CONV_CH), uc.dtype), conv_w[l])
        xp = xp + merge_groups(o_attn, bg, y_conv, g_attn_out[l], g_conv_out[l], w_out[l])
        kp_l.append(k); vp_l.append(v); fp_l.append(logf); cp_l.append(buf_p)

        q, k, v, logf, bg, uc = combined_projection(rms_norm(xs, g_mix[l]), w_in[l], b_f[l])
        k_past = cache_k[l][page_table].reshape(DEC_BATCH, n_past, N_ATTN_HEADS, HEAD_DIM)
        v_past = cache_v[l][page_table].reshape(DEC_BATCH, n_past, N_ATTN_HEADS, HEAD_DIM)
        f_past = cache_logf[l][page_table].reshape(DEC_BATCH, n_past, N_ATTN_HEADS)
        o_attn = fox_sample(q, k, v, logf, k_past, v_past, f_past)
        y_conv, buf_s = short_conv(uc, state_conv[l].astype(uc.dtype), conv_w[l])
        xs = xs + merge_groups(o_attn, bg, y_conv, g_attn_out[l], g_conv_out[l], w_out[l])
        ks_l.append(k); vs_l.append(v); fs_l.append(logf); cs_l.append(buf_s)

        xp = xp + 0.5 * swiglu(rms_norm(xp, g_ffn2[l]), w1_ffn2[l], w3_ffn2[l], w2_ffn2[l])
        xs = xs + 0.5 * swiglu(rms_norm(xs, g_ffn2[l]), w1_ffn2[l], w3_ffn2[l], w2_ffn2[l])

    y_prompt = rms_norm(xp, g_final)
    y_sample = rms_norm(xs, g_final)
    return (y_prompt, y_sample,
            jnp.stack(kp_l), jnp.stack(vp_l), jnp.stack(fp_l), jnp.stack(cp_l),
            jnp.stack(ks_l), jnp.stack(vs_l), jnp.stack(fs_l), jnp.stack(cs_l))
```

```python
import functools
import math

import jax
import jax.numpy as jnp
from jax import lax
from jax.experimental import pallas as pl
from jax.experimental.pallas import tpu as pltpu

F32 = jnp.float32
BF16 = jnp.bfloat16
EPS = 1e-6
NEG = -1e30
LANES = 128
ROW_TILE = 512
PAGES_PER_STEP = 8
VMEM_LIMIT = 56 * 1024 * 1024
AUG = 128
V_ROWS = 80


def _dot(a, b):
    return jnp.dot(a, b, preferred_element_type=F32)


def _dot_nt(a, b):
    return lax.dot_general(a, b, (((1,), (1,)), ((), ())), preferred_element_type=F32)


def _rms(x, g):
    return x * lax.rsqrt(jnp.mean(x * x, axis=-1, keepdims=True) + EPS) * g


def _split3(x):
    hi = x.astype(BF16).astype(F32)
    r1 = x - hi
    mid = r1.astype(BF16).astype(F32)
    lo = (r1 - mid).astype(BF16).astype(F32)
    return hi, mid, lo


def _const_spec(a):
    nd = a.ndim
    return pl.BlockSpec(a.shape, lambda *_: (0,) * nd, pipeline_mode=pl.Buffered(1))


def _ffn_body(*refs, merge, final):
    x_ref, g_ref, w1_ref, w3_ref, w2_ref = refs[:5]
    rest = refs[5:]
    x = x_ref[...]
    if merge:
        o_ref, cn_ref, gao_ref, woa_ref, woc_ref = rest[:5]
        rest = rest[5:]
        an = _rms(o_ref[...], gao_ref[...]).astype(BF16)
        x = x + (_dot(an, woa_ref[...]) + _dot(cn_ref[...], woc_ref[...]))
    if final:
        gfin_ref = rest[0]
        rest = rest[1:]
    out_ref = rest[0]
    h = _rms(x, g_ref[...]).astype(BF16)
    a = _dot(h, w1_ref[...])
    b = _dot(h, w3_ref[...])
    act = (a * jax.nn.sigmoid(a) * b).astype(BF16)
    y = x + 0.5 * _dot(act, w2_ref[...])
    if final:
        y = _rms(y, gfin_ref[...])
    out_ref[...] = y


def _ffn(x, g, w1, w3, w2, *, merge=None, g_final=None):
    n, d = x.shape
    tm = min(ROW_TILE, n)
    row = lambda i: (i, 0)
    args = [x, g, w1, w3, w2]
    specs = [pl.BlockSpec((tm, d), row)] + [_const_spec(a) for a in args[1:]]
    if merge is not None:
        o, cn, gao, woa, woc = merge
        args += [o, cn, gao, woa, woc]
        specs += [pl.BlockSpec((tm, o.shape[1]), row), pl.BlockSpec((tm, cn.shape[1]), row),
                  _const_spec(gao), _const_spec(woa), _const_spec(woc)]
    if g_final is not None:
        args.append(g_final)
        specs.append(_const_spec(g_final))
    return pl.pallas_call(
        functools.partial(_ffn_body, merge=merge is not None, final=g_final is not None),
        out_shape=jax.ShapeDtypeStruct((n, d), F32),
        grid=(n // tm,),
        in_specs=specs,
        out_specs=pl.BlockSpec((tm, d), row),
        compiler_params=pltpu.CompilerParams(dimension_semantics=("arbitrary",), vmem_limit_bytes=VMEM_LIMIT),
        name="ffn_merge" if merge is not None else "ffn",
    )(*args)


def _proj_common(x_ref, g_ref, w_ref, bf_ref, tri_ref, width, scale):
    a = width
    h = _rms(x_ref[...], g_ref[...]).astype(BF16)
    p = _dot(h, w_ref[...])
    q = p[:, 0:a] * scale
    k = p[:, a:2 * a]
    v = p[:, 2 * a:3 * a]
    bg = p[:, 3 * a:4 * a]
    uc = p[:, 4 * a:5 * a] * p[:, 5 * a:6 * a]
    z = p[:, 6 * a:6 * a + LANES] + bf_ref[...]
    lane = lax.broadcasted_iota(jnp.int32, z.shape, 1)
    logf = jnp.minimum(z, 0.0) - jnp.log1p(jnp.exp(-jnp.abs(z)))
    l3 = jnp.where(lane < 24, logf, 0.0)
    hi, mid, lo = _split3(l3)
    packed = jnp.where(lane < 8, hi, jnp.where(lane < 16, mid, lo)).astype(BF16)
    r = _dot(tri_ref[...], packed)
    c = r + pltpu.roll(r, LANES - 8, axis=1) + pltpu.roll(r, LANES - 16, axis=1)
    c = jnp.where(lane < 8, c, 0.0)
    return q, k, v, bg, uc, logf, c, lane


def _conv_branch(uc, prev1, prev2, bg, cw_ref, gco_ref):
    y = cw_ref[0:1, :] * prev2 + cw_ref[1:2, :] * prev1 + cw_ref[2:3, :] * uc
    return _rms(bg * y, gco_ref[...]).astype(BF16)


def _proj_prompt_body(x_ref, g_ref, w_ref, bf_ref, tri_ref, cw_ref, gco_ref, psel_ref, asel_ref,
                      k_ref, v_ref, logf_ref, qa_ref, ka_ref, va_ref, cn_ref, tail_ref,
                      carry_c, carry_u, *, n_heads, head_dim, scale):
    tm = x_ref.shape[0]
    width = n_heads * head_dim

    @pl.when(pl.program_id(1) == 0)
    def _():
        carry_c[...] = jnp.zeros_like(carry_c)
        carry_u[...] = jnp.zeros_like(carry_u)

    q, k, v, bg, uc, logf, c, lane = _proj_common(x_ref, g_ref, w_ref, bf_ref, tri_ref, width, scale)
    k_ref[...] = k
    v_ref[...] = v
    logf_ref[...] = logf[:, 0:n_heads]

    c = c + carry_c[...]
    carry_c[...] = c[tm - 1:tm, :]
    c_rep = jnp.where(lane < 8, c, jnp.where(lane < 16, pltpu.roll(c, 8, axis=1),
                                             jnp.where(lane < 24, pltpu.roll(c, 16, axis=1), 0.0)))
    chi, cmid, clo = _split3(c_rep)
    c3 = jnp.where(lane < 8, chi, jnp.where(lane < 16, cmid, jnp.where(lane < 24, clo,
                   jnp.where(lane == 24, 1.0, 0.0)))).astype(BF16)
    k_extra = _dot(c3, psel_ref[...])
    q_extra = _dot_nt(asel_ref[...], c3)

    q_t = q.T
    v_t = v.T
    lane_a = lax.broadcasted_iota(jnp.int32, (tm, AUG), 1)
    row8 = lax.broadcasted_iota(jnp.int32, (8, tm), 0)
    ones_row = jnp.where(row8 == 0, 1.0, 0.0)
    pad_q = jnp.zeros((AUG - head_dim - 8, tm), F32)
    pad_v = jnp.zeros((V_ROWS - head_dim - 8, tm), F32)
    for h in range(n_heads):
        blk = k[:, (h // 2) * AUG:(h // 2 + 1) * AUG]
        if h % 2:
            blk = pltpu.roll(blk, head_dim, axis=1)
        ka_ref[0, h] = jnp.where(lane_a < head_dim, blk, k_extra[:, h * AUG:(h + 1) * AUG]).astype(BF16)
        qa_ref[0, h] = jnp.concatenate(
            [q_t[h * head_dim:(h + 1) * head_dim], q_extra[h * 8:(h + 1) * 8], pad_q], axis=0).astype(BF16)
        va_ref[0, h, 0] = jnp.concatenate(
            [v_t[h * head_dim:(h + 1) * head_dim], ones_row, pad_v], axis=0).astype(BF16)

    rows = lax.broadcasted_iota(jnp.int32, uc.shape, 0)
    last1 = carry_u[7:8, :]
    last2 = carry_u[6:7, :]
    prev1 = jnp.where(rows == 0, last1, pltpu.roll(uc, 1, axis=0))
    prev2 = jnp.where(rows == 0, last2, jnp.where(rows == 1, last1, pltpu.roll(uc, 2, axis=0)))
    cn_ref[...] = _conv_branch(uc, prev1, prev2, bg, cw_ref, gco_ref)
    tail_ref[0] = uc[tm - 2:tm, :]
    carry_u[...] = uc[tm - 8:tm, :]


def _proj_sample_body(x_ref, g_ref, w_ref, bf_ref, tri_ref, cw_ref, gco_ref, s1_ref, s2_ref,
                      q_ref, k_ref, v_ref, logf_ref, c_ref, uc_ref, cn_ref, *, n_heads, head_dim, scale, seq):
    q, k, v, bg, uc, logf, c, _ = _proj_common(x_ref, g_ref, w_ref, bf_ref, tri_ref, n_heads * head_dim, scale)
    q_ref[...] = q
    k_ref[...] = k
    v_ref[...] = v
    logf_ref[...] = logf[:, 0:n_heads]
    c_ref[...] = c
    uc_ref[...] = uc
    t = lax.broadcasted_iota(jnp.int32, uc.shape, 0) % seq
    prev1 = jnp.where(t == 0, s1_ref[...], pltpu.roll(uc, 1, axis=0))
    prev2 = jnp.where(t < 2, s2_ref[...], pltpu.roll(uc, 2, axis=0))
    cn_ref[...] = _conv_branch(uc, prev1, prev2, bg, cw_ref, gco_ref)


def _selectors(n_heads):
    psel = jnp.zeros((LANES, n_heads * AUG), F32)
    asel = jnp.zeros((n_heads * 8, LANES), F32)
    for h in range(n_heads):
        base = h * AUG + 64
        psel = psel.at[24, base:base + 3].set(1.0)
        for part in range(3):
            psel = psel.at[part * 8 + h, base + 3 + part].set(-1.0)
            asel = asel.at[h * 8 + part, part * 8 + h].set(1.0)
        asel = asel.at[h * 8 + 3:h * 8 + 6, 24].set(1.0)
    return psel.astype(BF16), asel.astype(BF16)


def _proj_prompt(x, g, w, bf, cw, gco, *, batch, n_heads, head_dim):
    n, d = x.shape
    seq = n // batch
    tm = ROW_TILE
    nb = seq // tm
    width = n_heads * head_dim
    tri = jnp.tril(jnp.ones((tm, tm), F32)).astype(BF16)
    psel, asel = _selectors(n_heads)
    row = lambda b, i: (b * nb + i, 0)
    consts = [g, w, bf, tri, cw, gco, psel, asel]
    out_shape = (
        jax.ShapeDtypeStruct((n, width), F32),
        jax.ShapeDtypeStruct((n, width), F32),
        jax.ShapeDtypeStruct((n, n_heads), F32),
        jax.ShapeDtypeStruct((batch, n_heads, AUG, seq), BF16),
        jax.ShapeDtypeStruct((batch, n_heads, seq, AUG), BF16),
        jax.ShapeDtypeStruct((batch, n_heads, nb, V_ROWS, tm), BF16),
        jax.ShapeDtypeStruct((n, width), BF16),
        jax.ShapeDtypeStruct((batch, 2, width), F32),
    )
    out_specs = (
        pl.BlockSpec((tm, width), row),
        pl.BlockSpec((tm, width), row),
        pl.BlockSpec((tm, n_heads), row),
        pl.BlockSpec((1, n_heads, AUG, tm), lambda b, i: (b, 0, 0, i)),
        pl.BlockSpec((1, n_heads, tm, AUG), lambda b, i: (b, 0, i, 0)),
        pl.BlockSpec((1, n_heads, 1, V_ROWS, tm), lambda b, i: (b, 0, i, 0, 0)),
        pl.BlockSpec((tm, width), row),
        pl.BlockSpec((1, 2, width), lambda b, i: (b, 0, 0)),
    )
    return pl.pallas_call(
        functools.partial(_proj_prompt_body, n_heads=n_heads, head_dim=head_dim, scale=1.0 / math.sqrt(head_dim)),
        out_shape=out_shape,
        grid=(batch, nb),
        in_specs=[pl.BlockSpec((tm, d), row)] + [_const_spec(a) for a in consts],
        out_specs=out_specs,
        scratch_shapes=[pltpu.VMEM((1, LANES), F32), pltpu.VMEM((8, width), F32)],
        compiler_params=pltpu.CompilerParams(dimension_semantics=("arbitrary", "arbitrary"),
                                             vmem_limit_bytes=VMEM_LIMIT),
        name="proj_prompt",
    )(x, *consts)


def _proj_sample(x, g, w, bf, cw, gco, s1, s2, *, seq, n_heads, head_dim):
    n, d = x.shape
    width = n_heads * head_dim
    r = jnp.arange(n)
    tri = ((r[None, :] <= r[:, None]) & (r[None, :] // seq == r[:, None] // seq)).astype(BF16)
    args = [x, g, w, bf, tri, cw, gco, s1, s2]
    shapes = [(n, width)] * 3 + [(n, n_heads), (n, LANES), (n, width)]
    out_shape = tuple(jax.ShapeDtypeStruct(s, F32) for s in shapes) + (jax.ShapeDtypeStruct((n, width), BF16),)
    return pl.pallas_call(
        functools.partial(_proj_sample_body, n_heads=n_heads, head_dim=head_dim,
                          scale=1.0 / math.sqrt(head_dim), seq=seq),
        out_shape=out_shape,
        compiler_params=pltpu.CompilerParams(vmem_limit_bytes=VMEM_LIMIT),
        name="proj_sample",
    )(*args)


def _attn_prompt_body(qa_ref, ka_ref, va_ref, o_ref, *, head_dim):
    i = pl.program_id(2)
    tq = qa_ref.shape[3]
    tk = tq
    sub = lax.broadcasted_iota(jnp.int32, (tk, tq), 0)
    lan = lax.broadcasted_iota(jnp.int32, (tk, tq), 1)
    causal = sub <= lan
    outs = []
    for hh in range(qa_ref.shape[1]):
        q_t = qa_ref[0, hh]

        def scores(j, hh=hh, q_t=q_t):
            k_j = ka_ref[0, hh, pl.ds(pl.multiple_of(j * tk, tk), tk), :]
            return _dot(k_j, q_t)

        s = jnp.where(causal, scores(i), NEG)
        m = jnp.max(s, axis=0, keepdims=True)
        p = jnp.exp(s - m).astype(BF16)
        acc = _dot(va_ref[0, hh, i], p)

        def body(j, carry, hh=hh, scores=scores):
            m, acc = carry
            s = scores(j)
            m_new = jnp.maximum(m, jnp.max(s, axis=0, keepdims=True))
            p = jnp.exp(s - m_new).astype(BF16)
            return m_new, jnp.exp(m - m_new) * acc + _dot(va_ref[0, hh, j], p)

        m, acc = lax.fori_loop(0, i, body, (m, acc))
        outs.append(acc[0:head_dim] / acc[head_dim:head_dim + 1])
    o_ref[0] = jnp.concatenate(outs, axis=0).T


def _attn_prompt(qa, ka, va, *, head_dim):
    batch, n_heads, _, seq = qa.shape
    nb, tq = va.shape[2], va.shape[4]
    hps = LANES // head_dim
    return pl.pallas_call(
        functools.partial(_attn_prompt_body, head_dim=head_dim),
        out_shape=jax.ShapeDtypeStruct((batch, seq, n_heads * head_dim), F32),
        grid=(batch, n_heads // hps, nb),
        in_specs=[
            pl.BlockSpec((1, hps, AUG, tq), lambda b, g, i: (b, g, 0, i)),
            pl.BlockSpec((1, hps, seq, AUG), lambda b, g, i: (b, g, 0, 0)),
            pl.BlockSpec((1, hps, nb, V_ROWS, tq), lambda b, g, i: (b, g, 0, 0, 0)),
        ],
        out_specs=pl.BlockSpec((1, tq, hps * head_dim), lambda b, g, i: (b, i, g)),
        compiler_params=pltpu.CompilerParams(dimension_semantics=("arbitrary",) * 3, vmem_limit_bytes=VMEM_LIMIT),
        name="attn_prompt",
    )(qa, ka, va)


def _attn_sample_body(pt_ref, qbd_ref, cq_ref, knew_ref, vnew_ref, cnew_ref, *refs, n_pages_step, seq, n_heads,
                      head_dim):
    g = n_pages_step
    kt_refs, vt_refs, lf_refs = refs[:g], refs[g:2 * g], refs[2 * g:3 * g]
    o_ref, m_ref, l_ref, acc_ref, carry_ref = refs[3 * g:]
    step = pl.program_id(1)
    qb = qbd_ref[0]
    cq = cq_ref[0]
    n_new = knew_ref.shape[1]

    @pl.when(step == 0)
    def _():
        s = _dot_nt(qb, knew_ref[0]) + cq[:, 0:n_new] - cnew_ref[0]
        t_row = lax.broadcasted_iota(jnp.int32, s.shape, 0) // n_heads
        j = lax.broadcasted_iota(jnp.int32, s.shape, 1)
        s = jnp.where((j <= t_row) & (j < seq), s, NEG)
        m = jnp.max(s, axis=1, keepdims=True)
        p = jnp.exp(s - m)
        m_ref[...] = m
        l_ref[...] = jnp.sum(p, axis=1, keepdims=True)
        acc_ref[...] = _dot(p.astype(BF16), vnew_ref[0])
        carry_ref[...] = jnp.zeros_like(carry_ref)

    lane = lax.broadcasted_iota(jnp.int32, (n_heads, LANES), 1)
    carry = carry_ref[...]
    s_parts = []
    for p_i in range(g):
        lf = lf_refs[p_i][0]
        v = lf
        sh = 1
        while sh < LANES:
            v = v + jnp.where(lane + sh < LANES, pltpu.roll(v, LANES - sh, axis=1), 0.0)
            sh *= 2
        suffix = v - lf + carry
        carry = carry + v[:, 0:1]
        bias = jnp.concatenate([suffix] * seq, axis=0) + cq
        s_parts.append(_dot(qb, kt_refs[p_i][0].astype(BF16)) + bias)
    carry_ref[...] = carry
    s = jnp.concatenate(s_parts, axis=1)
    m_prev = m_ref[...]
    m_new = jnp.maximum(m_prev, jnp.max(s, axis=1, keepdims=True))
    p = jnp.exp(s - m_new)
    alpha = jnp.exp(m_prev - m_new)
    l_ref[...] = alpha * l_ref[...] + jnp.sum(p, axis=1, keepdims=True)
    pb = p.astype(BF16)
    pv = _dot_nt(pb[:, 0:LANES], vt_refs[0][0].astype(BF16))
    for p_i in range(1, g):
        pv = pv + _dot_nt(pb[:, p_i * LANES:(p_i + 1) * LANES], vt_refs[p_i][0].astype(BF16))
    acc_ref[...] = alpha * acc_ref[...] + pv
    m_ref[...] = m_new

    @pl.when(step == pl.num_programs(1) - 1)
    def _():
        o = acc_ref[...] / l_ref[...]
        r = lax.broadcasted_iota(jnp.int32, o.shape, 0)
        col = lax.broadcasted_iota(jnp.int32, o.shape, 1)
        o = jnp.where(col // head_dim == r % n_heads, o, 0.0)
        o_ref[0] = jnp.sum(o.reshape(seq, n_heads, o.shape[1]), axis=1)


def _attn_sample(page_table, qbd, cq, knew, vnew, cnew, kt_pool, vt_pool, lf_pool, *, seq, n_heads, head_dim):
    dec_batch, n_pages = page_table.shape
    g = PAGES_PER_STEP
    rows, width = qbd.shape[1], qbd.shape[2]
    page = kt_pool.shape[2]

    def per_batch(a):
        return pl.BlockSpec((1,) + a.shape[1:], lambda b, s, pt: (b,) + (0,) * (a.ndim - 1))

    def paged(a, p_i):
        return pl.BlockSpec((1,) + a.shape[1:],
                            lambda b, s, pt: (pt[b * n_pages + n_pages - 1 - (s * g + p_i)], 0, 0))

    in_specs = [per_batch(a) for a in (qbd, cq, knew, vnew, cnew)]
    in_specs += [paged(kt_pool, p_i) for p_i in range(g)]
    in_specs += [paged(vt_pool, p_i) for p_i in range(g)]
    in_specs += [paged(lf_pool, p_i) for p_i in range(g)]
    return pl.pallas_call(
        functools.partial(_attn_sample_body, n_pages_step=g, seq=seq, n_heads=n_heads, head_dim=head_dim),
        out_shape=jax.ShapeDtypeStruct((dec_batch, seq, width), F32),
        grid_spec=pltpu.PrefetchScalarGridSpec(
            num_scalar_prefetch=1,
            grid=(dec_batch, n_pages // g),
            in_specs=in_specs,
            out_specs=pl.BlockSpec((1, seq, width), lambda b, s, pt: (b, 0, 0)),
            scratch_shapes=[pltpu.VMEM((rows, 1), F32), pltpu.VMEM((rows, 1), F32),
                            pltpu.VMEM((rows, width), F32), pltpu.VMEM((n_heads, 1), F32)],
        ),
        compiler_params=pltpu.CompilerParams(dimension_semantics=("arbitrary", "arbitrary"),
                                             vmem_limit_bytes=VMEM_LIMIT),
        name="attn_sample",
    )(page_table.reshape(-1), qbd, cq, knew, vnew, cnew, *([kt_pool] * g), *([vt_pool] * g), *([lf_pool] * g))


def kernel(x_prompt, x_sample, cache_k, cache_v, cache_logf, state_conv, page_table, g_ffn1, w1_ffn1, w3_ffn1,
           w2_ffn1, g_mix, w_in, b_f, conv_w, g_attn_out, g_conv_out, w_out, g_ffn2, w1_ffn2, w3_ffn2, w2_ffn2,
           g_final):
    batch, seq, d_model = x_prompt.shape
    dec_batch, dec_seq, _ = x_sample.shape
    depth, n_pool, page, n_heads, head_dim = cache_k.shape
    width = n_heads * head_dim
    assert depth == 1 and seq % ROW_TILE == 0 and page == LANES and 2 * head_dim == LANES
    assert page_table.shape[1] % PAGES_PER_STEP == 0 and w_in.shape[2] == 6 * width + n_heads

    xp = x_prompt.reshape(batch * seq, d_model)
    xs = x_sample.reshape(dec_batch * dec_seq, d_model)
    row = lambda a: a.reshape(1, -1)
    l = 0

    bw = lambda a: a.astype(BF16)
    wi = w_in[l]
    wf = jnp.tile(wi[:, 3 * width:3 * width + n_heads], (1, 3))
    w_proj = bw(jnp.concatenate([wi[:, :3 * width], wi[:, 3 * width + n_heads:],
                                 jnp.pad(wf, ((0, 0), (0, LANES - 3 * n_heads)))], axis=1))
    bf = jnp.pad(jnp.tile(b_f[l], 3), (0, LANES - 3 * n_heads)).reshape(1, LANES)
    ffn1 = (row(g_ffn1[l]), bw(w1_ffn1[l]), bw(w3_ffn1[l]), bw(w2_ffn1[l]))
    ffn2 = (row(g_ffn2[l]), bw(w1_ffn2[l]), bw(w3_ffn2[l]), bw(w2_ffn2[l]))
    wo = bw(w_out[l])
    merge_w = (row(g_attn_out[l]), wo[:width], wo[width:])
    gmix, cw, gco, gfin = row(g_mix[l]), conv_w[l], row(g_conv_out[l]), row(g_final)

    xp1 = _ffn(xp, *ffn1)
    k_p, v_p, logf_p, qa, ka, va, cn_p, tail_p = _proj_prompt(
        xp1, gmix, w_proj, bf, cw, gco, batch=batch, n_heads=n_heads, head_dim=head_dim)
    o_p = _attn_prompt(qa, ka, va, head_dim=head_dim).reshape(batch * seq, width)
    y_p = _ffn(xp1, *ffn2, merge=(o_p, cn_p) + merge_w, g_final=gfin)

    xs1 = _ffn(xs, *ffn1)
    st = state_conv[l]
    s1 = jnp.repeat(st[:, 1:2], dec_seq, axis=1).reshape(dec_batch * dec_seq, width)
    s2 = jnp.pad(st, ((0, 0), (0, dec_seq - 2), (0, 0))).reshape(dec_batch * dec_seq, width)
    q_s, k_s, v_s, logf_s, c_s, uc_s, cn_s = _proj_sample(
        xs1, gmix, w_proj, bf, cw, gco, s1, s2, seq=dec_seq, n_heads=n_heads, head_dim=head_dim)

    eye = jnp.eye(n_heads, dtype=F32)
    q4 = q_s.reshape(dec_batch, dec_seq, n_heads, 1, head_dim)
    qbd = bw((q4 * eye[None, None, :, :, None]).reshape(dec_batch, dec_seq * n_heads, width))
    c4 = c_s[:, :n_heads].reshape(dec_batch, dec_seq, n_heads)
    cq = jnp.broadcast_to(c4.reshape(dec_batch, dec_seq * n_heads, 1), (dec_batch, dec_seq * n_heads, LANES))
    n_new = 16
    cnew = jnp.broadcast_to(jnp.transpose(c4, (0, 2, 1))[:, None], (dec_batch, dec_seq, n_heads, dec_seq))
    cnew = jnp.pad(cnew.reshape(dec_batch, dec_seq * n_heads, dec_seq), ((0, 0), (0, 0), (0, n_new - dec_seq)))
    pad_new = lambda a: bw(jnp.pad(a.reshape(dec_batch, dec_seq, width), ((0, 0), (0, n_new - dec_seq), (0, 0))))
    kt_pool = jnp.transpose(cache_k[l], (0, 2, 3, 1)).reshape(n_pool, width, page)
    vt_pool = jnp.transpose(cache_v[l], (0, 2, 3, 1)).reshape(n_pool, width, page)
    lf_pool = jnp.transpose(cache_logf[l], (0, 2, 1))
    o_s = _attn_sample(page_table, qbd, cq, pad_new(k_s), pad_new(v_s), cnew, kt_pool, vt_pool, lf_pool,
                       seq=dec_seq, n_heads=n_heads, head_dim=head_dim).reshape(dec_batch * dec_seq, width)
    y_s = _ffn(xs1, *ffn2, merge=(o_s, cn_s) + merge_w, g_final=gfin)

    hd = (n_heads, head_dim)
    return (
        y_p.reshape(batch, seq, d_model),
        y_s.reshape(dec_batch, dec_seq, d_model),
        k_p.reshape(1, batch, seq, *hd),
        v_p.reshape(1, batch, seq, *hd),
        logf_p.reshape(1, batch, seq, n_heads),
        tail_p.reshape(1, batch, 2, width),
        k_s.reshape(1, dec_batch, dec_seq, *hd),
        v_s.reshape(1, dec_batch, dec_seq, *hd),
        logf_s.reshape(1, dec_batch, dec_seq, n_heads),
        uc_s.reshape(1, dec_batch, dec_seq, width)[:, :, dec_seq - 2:],
    )
```

```python
import functools
import math

import jax
import jax.numpy as jnp
from jax import lax
from jax.experimental import pallas as pl
from jax.experimental.pallas import tpu as pltpu

F32 = jnp.float32
BF16 = jnp.bfloat16
EPS = 1e-6
NEG = -1e30
LANES = 128
ROW_TILE = 512
PAGES_PER_STEP = 16
HEADS_PER_STEP = 8
SHIFT_SLACK = 64.0
SUFFIX_ROWS = 2048
LOG2E = 1.4426950408889634
VMEM_LIMIT = 56 * 1024 * 1024
AUG = 128
V_ROWS = 80


def _dot(a, b):
    return jnp.dot(a, b, preferred_element_type=F32)


def _dot_nt(a, b):
    return lax.dot_general(a, b, (((1,), (1,)), ((), ())), preferred_element_type=F32)


def _rms(x, g):
    return x * lax.rsqrt(jnp.mean(x * x, axis=-1, keepdims=True) + EPS) * g


def _split3(x):
    hi = x.astype(BF16).astype(F32)
    r1 = x - hi
    mid = r1.astype(BF16).astype(F32)
    lo = (r1 - mid).astype(BF16).astype(F32)
    return hi, mid, lo


def _const_spec(a):
    nd = a.ndim
    return pl.BlockSpec(a.shape, lambda *_: (0,) * nd, pipeline_mode=pl.Buffered(1))


def _ffn_body(*refs, merge, final):
    x_ref, g_ref, w1_ref, w3_ref, w2_ref = refs[:5]
    rest = refs[5:]
    x = x_ref[...]
    if merge:
        o_ref, cn_ref, gao_ref, woa_ref, woc_ref = rest[:5]
        rest = rest[5:]
        an = _rms(o_ref[...], gao_ref[...]).astype(BF16)
        x = x + (_dot(an, woa_ref[...]) + _dot(cn_ref[...], woc_ref[...]))
    if final:
        gfin_ref = rest[0]
        rest = rest[1:]
    out_ref = rest[0]
    h = _rms(x, g_ref[...]).astype(BF16)
    a = _dot(h, w1_ref[...])
    b = _dot(h, w3_ref[...])
    act = (a * jax.nn.sigmoid(a) * b).astype(BF16)
    y = x + 0.5 * _dot(act, w2_ref[...])
    if final:
        y = _rms(y, gfin_ref[...])
    out_ref[...] = y


def _ffn(x, g, w1, w3, w2, *, merge=None, g_final=None):
    n, d = x.shape
    tm = min(ROW_TILE, n)
    row = lambda i: (i, 0)
    args = [x, g, w1, w3, w2]
    specs = [pl.BlockSpec((tm, d), row)] + [_const_spec(a) for a in args[1:]]
    if merge is not None:
        o, cn, gao, woa, woc = merge
        args += [o, cn, gao, woa, woc]
        specs += [pl.BlockSpec((tm, o.shape[1]), row), pl.BlockSpec((tm, cn.shape[1]), row),
                  _const_spec(gao), _const_spec(woa), _const_spec(woc)]
    if g_final is not None:
        args.append(g_final)
        specs.append(_const_spec(g_final))
    return pl.pallas_call(
        functools.partial(_ffn_body, merge=merge is not None, final=g_final is not None),
        out_shape=jax.ShapeDtypeStruct((n, d), F32),
        grid=(n // tm,),
        in_specs=specs,
        out_specs=pl.BlockSpec((tm, d), row),
        compiler_params=pltpu.CompilerParams(dimension_semantics=("arbitrary",), vmem_limit_bytes=VMEM_LIMIT),
        name="ffn_merge" if merge is not None else "ffn",
    )(*args)


def _proj_common(x_ref, g_ref, w_ref, bf_ref, tri_ref, width, scale):
    a = width
    h = _rms(x_ref[...], g_ref[...]).astype(BF16)
    p = _dot(h, w_ref[...])
    q = p[:, 0:a] * scale
    k = p[:, a:2 * a]
    v = p[:, 2 * a:3 * a]
    bg = p[:, 3 * a:4 * a]
    uc = p[:, 4 * a:5 * a] * p[:, 5 * a:6 * a]
    z = p[:, 6 * a:6 * a + LANES] + bf_ref[...]
    lane = lax.broadcasted_iota(jnp.int32, z.shape, 1)
    logf = jnp.minimum(z, 0.0) - jnp.log1p(jnp.exp(-jnp.abs(z)))
    l3 = jnp.where(lane < 24, logf, 0.0)
    hi, mid, lo = _split3(l3)
    packed = jnp.where(lane < 8, hi, jnp.where(lane < 16, mid, lo)).astype(BF16)
    r = _dot(tri_ref[...], packed)
    c = r + pltpu.roll(r, LANES - 8, axis=1) + pltpu.roll(r, LANES - 16, axis=1)
    c = jnp.where(lane < 8, c, 0.0)
    return q, k, v, bg, uc, logf, c, lane


def _conv_branch(uc, prev1, prev2, bg, cw_ref, gco_ref):
    y = cw_ref[0:1, :] * prev2 + cw_ref[1:2, :] * prev1 + cw_ref[2:3, :] * uc
    return _rms(bg * y, gco_ref[...]).astype(BF16)


def _proj_prompt_body(x_ref, g_ref, w_ref, bf_ref, tri_ref, cw_ref, gco_ref, psel_ref, asel_ref, hsel_ref,
                      kt_ref, vt_ref, logf_ref, qa_ref, ka_ref, va_ref, cn_ref, tail_ref,
                      carry_c, carry_u, *, n_heads, head_dim, scale):
    tm = x_ref.shape[0]
    width = n_heads * head_dim

    @pl.when(pl.program_id(1) == 0)
    def _():
        carry_c[...] = jnp.zeros_like(carry_c)
        carry_u[...] = jnp.zeros_like(carry_u)

    q, k, v, bg, uc, logf, c, lane = _proj_common(x_ref, g_ref, w_ref, bf_ref, tri_ref, width, scale * LOG2E)
    q_t = q.T
    v_t = v.T
    kt_ref[0] = k.T
    vt_ref[0] = v_t
    logf_ref[0] = logf.T[0:n_heads]

    c = c + carry_c[...]
    carry_c[...] = c[tm - 1:tm, :]
    c2 = c * LOG2E

    def packed_parts(x):
        rep = jnp.where(lane < 8, x, jnp.where(lane < 16, pltpu.roll(x, 8, axis=1),
                                               jnp.where(lane < 24, pltpu.roll(x, 16, axis=1), 0.0)))
        hi, mid, lo = _split3(rep)
        return jnp.where(lane < 8, hi, jnp.where(lane < 16, mid, jnp.where(lane < 24, lo,
                         jnp.where(lane == 24, 1.0, 0.0)))).astype(BF16)

    shift = _dot((q * k).astype(BF16), hsel_ref[...])
    k_extra = _dot(packed_parts(c2), psel_ref[...])
    q_extra = _dot_nt(asel_ref[...], packed_parts(c2 - shift))

    lane_a = lax.broadcasted_iota(jnp.int32, (tm, AUG), 1)
    row8 = lax.broadcasted_iota(jnp.int32, (8, tm), 0)
    ones_row = jnp.where(row8 == 0, 1.0, 0.0)
    pad_q = jnp.zeros((AUG - head_dim - 8, tm), F32)
    pad_v = jnp.zeros((V_ROWS - head_dim - 8, tm), F32)
    for h in range(n_heads):
        blk = k[:, (h // 2) * AUG:(h // 2 + 1) * AUG]
        if h % 2:
            blk = pltpu.roll(blk, head_dim, axis=1)
        ka_ref[0, h] = jnp.where(lane_a < head_dim, blk, k_extra[:, h * AUG:(h + 1) * AUG]).astype(BF16)
        qa_ref[0, h] = jnp.concatenate(
            [q_t[h * head_dim:(h + 1) * head_dim], q_extra[h * 8:(h + 1) * 8], pad_q], axis=0).astype(BF16)
        va_ref[0, h, 0] = jnp.concatenate(
            [v_t[h * head_dim:(h + 1) * head_dim], ones_row, pad_v], axis=0).astype(BF16)

    rows = lax.broadcasted_iota(jnp.int32, uc.shape, 0)
    last1 = carry_u[7:8, :]
    last2 = carry_u[6:7, :]
    prev1 = jnp.where(rows == 0, last1, pltpu.roll(uc, 1, axis=0))
    prev2 = jnp.where(rows == 0, last2, jnp.where(rows == 1, last1, pltpu.roll(uc, 2, axis=0)))
    cn_ref[...] = _conv_branch(uc, prev1, prev2, bg, cw_ref, gco_ref)
    tail_ref[0] = uc[tm - 2:tm, :]
    carry_u[...] = uc[tm - 8:tm, :]


def _proj_sample_body(x_ref, g_ref, w_ref, bf_ref, tri_ref, cw_ref, gco_ref, s1_ref, s2_ref,
                      q_ref, k_ref, v_ref, logf_ref, c_ref, uc_ref, cn_ref, *, n_heads, head_dim, scale, seq):
    q, k, v, bg, uc, logf, c, _ = _proj_common(x_ref, g_ref, w_ref, bf_ref, tri_ref, n_heads * head_dim, scale)
    q_ref[...] = q
    k_ref[...] = k
    v_ref[...] = v
    logf_ref[...] = logf[:, 0:n_heads]
    c_ref[...] = c
    uc_ref[...] = uc
    t = lax.broadcasted_iota(jnp.int32, uc.shape, 0) % seq
    prev1 = jnp.where(t == 0, s1_ref[...], pltpu.roll(uc, 1, axis=0))
    prev2 = jnp.where(t < 2, s2_ref[...], pltpu.roll(uc, 2, axis=0))
    cn_ref[...] = _conv_branch(uc, prev1, prev2, bg, cw_ref, gco_ref)


def _selectors(n_heads):
    psel = jnp.zeros((LANES, n_heads * AUG), F32)
    asel = jnp.zeros((n_heads * 8, LANES), F32)
    for h in range(n_heads):
        base = h * AUG + 64
        psel = psel.at[24, base:base + 3].set(1.0)
        for part in range(3):
            psel = psel.at[part * 8 + h, base + 3 + part].set(-1.0)
            asel = asel.at[h * 8 + part, part * 8 + h].set(1.0)
        asel = asel.at[h * 8 + 3:h * 8 + 6, 24].set(1.0)
    return psel.astype(BF16), asel.astype(BF16)


def _proj_prompt(x, g, w, bf, cw, gco, *, batch, n_heads, head_dim):
    n, d = x.shape
    seq = n // batch
    tm = ROW_TILE
    nb = seq // tm
    width = n_heads * head_dim
    tri = jnp.tril(jnp.ones((tm, tm), F32)).astype(BF16)
    psel, asel = _selectors(n_heads)
    hsel = (jnp.arange(width)[:, None] // head_dim == jnp.arange(LANES)[None, :]).astype(BF16)
    row = lambda b, i: (b * nb + i, 0)
    consts = [g, w, bf, tri, cw, gco, psel, asel, hsel]
    out_shape = (
        jax.ShapeDtypeStruct((batch, width, seq), F32),
        jax.ShapeDtypeStruct((batch, width, seq), F32),
        jax.ShapeDtypeStruct((batch, n_heads, seq), F32),
        jax.ShapeDtypeStruct((batch, n_heads, AUG, seq), BF16),
        jax.ShapeDtypeStruct((batch, n_heads, seq, AUG), BF16),
        jax.ShapeDtypeStruct((batch, n_heads, nb, V_ROWS, tm), BF16),
        jax.ShapeDtypeStruct((n, width), BF16),
        jax.ShapeDtypeStruct((batch, 2, width), F32),
    )
    out_specs = (
        pl.BlockSpec((1, width, tm), lambda b, i: (b, 0, i)),
        pl.BlockSpec((1, width, tm), lambda b, i: (b, 0, i)),
        pl.BlockSpec((1, n_heads, tm), lambda b, i: (b, 0, i)),
        pl.BlockSpec((1, n_heads, AUG, tm), lambda b, i: (b, 0, 0, i)),
        pl.BlockSpec((1, n_heads, tm, AUG), lambda b, i: (b, 0, i, 0)),
        pl.BlockSpec((1, n_heads, 1, V_ROWS, tm), lambda b, i: (b, 0, i, 0, 0)),
        pl.BlockSpec((tm, width), row),
        pl.BlockSpec((1, 2, width), lambda b, i: (b, 0, 0)),
    )
    return pl.pallas_call(
        functools.partial(_proj_prompt_body, n_heads=n_heads, head_dim=head_dim, scale=1.0 / math.sqrt(head_dim)),
        out_shape=out_shape,
        grid=(batch, nb),
        in_specs=[pl.BlockSpec((tm, d), row)] + [_const_spec(a) for a in consts],
        out_specs=out_specs,
        scratch_shapes=[pltpu.VMEM((1, LANES), F32), pltpu.VMEM((8, width), F32)],
        compiler_params=pltpu.CompilerParams(dimension_semantics=("arbitrary", "arbitrary"),
                                             vmem_limit_bytes=VMEM_LIMIT),
        name="proj_prompt",
    )(x, *consts)


def _proj_sample(x, g, w, bf, cw, gco, s1, s2, *, seq, n_heads, head_dim):
    n, d = x.shape
    width = n_heads * head_dim
    r = jnp.arange(n)
    tri = ((r[None, :] <= r[:, None]) & (r[None, :] // seq == r[:, None] // seq)).astype(BF16)
    args = [x, g, w, bf, tri, cw, gco, s1, s2]
    shapes = [(n, width)] * 3 + [(n, n_heads), (n, LANES), (n, width)]
    out_shape = tuple(jax.ShapeDtypeStruct(s, F32) for s in shapes) + (jax.ShapeDtypeStruct((n, width), BF16),)
    return pl.pallas_call(
        functools.partial(_proj_sample_body, n_heads=n_heads, head_dim=head_dim,
                          scale=1.0 / math.sqrt(head_dim), seq=seq),
        out_shape=out_shape,
        compiler_params=pltpu.CompilerParams(vmem_limit_bytes=VMEM_LIMIT),
        name="proj_sample",
    )(*args)


def _attn_prompt_body(qa_ref, ka_ref, va_ref, o_ref, acc_ref, s_ref, ot_ref, *, head_dim):
    i = pl.program_id(2)
    n_h = qa_ref.shape[1]
    tq = qa_ref.shape[3]
    tk = tq
    sub = lax.broadcasted_iota(jnp.int32, (tk, tq), 0)
    lan = lax.broadcasted_iota(jnp.int32, (tk, tq), 1)
    causal = sub <= lan

    def scores(j, hh):
        k_j = ka_ref[0, hh, pl.ds(pl.multiple_of(j * tk, tk), tk), :]
        return _dot(k_j, qa_ref[0, hh])

    def normalised(acc):
        return acc[0:head_dim] / acc[head_dim:head_dim + 1]

    top = []
    s_next = scores(i, 0)
    for hh in range(n_h):
        s = jnp.where(causal, s_next, NEG)
        s_next = scores(i, hh + 1) if hh + 1 < n_h else scores(0, 0)
        top.append(jnp.max(s, axis=0, keepdims=True))
        acc_ref[hh] = _dot(va_ref[0, hh, i], jnp.exp2(s).astype(BF16))
    s_ref[...] = s_next

    def one_pass_block(j, top):
        s_next = s_ref[...]
        new_top = []
        for hh in range(n_h):
            s = s_next
            s_next = scores(j, hh + 1) if hh + 1 < n_h else scores(j + 1, 0)
            new_top.append(jnp.maximum(top[hh], jnp.max(s, axis=0, keepdims=True)))
            acc_ref[hh] += _dot(va_ref[0, hh, j], jnp.exp2(s).astype(BF16))
        s_ref[...] = s_next
        return tuple(new_top)

    top = lax.fori_loop(0, i, one_pass_block, tuple(top))
    o_ref[0] = jnp.concatenate([normalised(acc_ref[hh]) for hh in range(n_h)], axis=0).T
    hi, lo = top[0], top[0]
    for t in top[1:]:
        hi, lo = jnp.maximum(hi, t), jnp.minimum(lo, t)

    @pl.when((jnp.max(hi) > SHIFT_SLACK) | (jnp.min(lo) < -SHIFT_SLACK))
    def _():
        def online_head(hh, _):
            q_t = qa_ref[0, hh]

            def sc(j):
                return _dot(ka_ref[0, hh, pl.ds(pl.multiple_of(j * tk, tk), tk), :], q_t)

            s = jnp.where(causal, sc(i), NEG)
            m = jnp.max(s, axis=0, keepdims=True)
            acc = _dot(va_ref[0, hh, i], jnp.exp2(s - m).astype(BF16))

            def body(j, carry):
                m, acc = carry
                s = sc(j)
                m_new = jnp.maximum(m, jnp.max(s, axis=0, keepdims=True))
                p = jnp.exp2(s - m_new).astype(BF16)
                return m_new, jnp.exp2(m - m_new) * acc + _dot(va_ref[0, hh, j], p)

            m, acc = lax.fori_loop(0, i, body, (m, acc))
            ot_ref[pl.ds(pl.multiple_of(hh * head_dim, head_dim), head_dim), :] = normalised(acc)
            return 0

        lax.fori_loop(0, n_h, online_head, 0)
        o_ref[0] = ot_ref[...].T


def _attn_prompt(qa, ka, va, *, head_dim):
    batch, n_heads, _, seq = qa.shape
    nb, tq = va.shape[2], va.shape[4]
    hps = HEADS_PER_STEP
    resident = dict(pipeline_mode=pl.Buffered(1))
    return pl.pallas_call(
        functools.partial(_attn_prompt_body, head_dim=head_dim),
        out_shape=jax.ShapeDtypeStruct((batch, seq, n_heads * head_dim), F32),
        grid=(batch, n_heads // hps, nb),
        in_specs=[
            pl.BlockSpec((1, hps, AUG, tq), lambda b, g, i: (b, g, 0, i)),
            pl.BlockSpec((1, hps, seq, AUG), lambda b, g, i: (b, g, 0, 0), **resident),
            pl.BlockSpec((1, hps, nb, V_ROWS, tq), lambda b, g, i: (b, g, 0, 0, 0), **resident),
        ],
        out_specs=pl.BlockSpec((1, tq, hps * head_dim), lambda b, g, i: (b, i, g)),
        scratch_shapes=[pltpu.VMEM((hps, V_ROWS, tq), F32), pltpu.VMEM((tq, tq), F32),
                        pltpu.VMEM((hps * head_dim, tq), F32)],
        compiler_params=pltpu.CompilerParams(dimension_semantics=("arbitrary",) * 3, vmem_limit_bytes=VMEM_LIMIT),
        name="attn_prompt",
    )(qa, ka, va)


def _suffix_body(x_ref, rhs_ref, o_ref):
    hi, mid, lo = _split3(x_ref[...])
    rhs = rhs_ref[...]
    o_ref[...] = _dot(hi.astype(BF16), rhs) + _dot(mid.astype(BF16), rhs) + _dot(lo.astype(BF16), rhs)


def _suffix_pool(lf_pool):
    n_pool, n_heads, page = lf_pool.shape
    rows = n_pool * n_heads
    tr = SUFFIX_ROWS
    assert rows % tr == 0
    r = jnp.arange(page)
    rhs = jnp.concatenate([(r[:, None] > r[None, :]).astype(BF16), jnp.ones((page, page), BF16)], axis=1)
    out = pl.pallas_call(
        _suffix_body,
        out_shape=jax.ShapeDtypeStruct((rows, 2 * page), F32),
        grid=(rows // tr,),
        in_specs=[pl.BlockSpec((tr, page), lambda i: (i, 0)), _const_spec(rhs)],
        out_specs=pl.BlockSpec((tr, 2 * page), lambda i: (i, 0)),
        compiler_params=pltpu.CompilerParams(dimension_semantics=("arbitrary",), vmem_limit_bytes=VMEM_LIMIT),
        name="suffix_pool",
    )(lf_pool.reshape(rows, page), rhs)
    return out.reshape(n_pool, n_heads, 2 * page)


def _attn_sample_body(pt_ref, qbd_ref, cq_ref, knew_ref, vnew_ref, cnew_ref, *refs, n_pages_step, seq, n_heads,
                      head_dim):
    g = n_pages_step
    kt_refs, vt_refs, lf_refs = refs[:g], refs[g:2 * g], refs[2 * g:3 * g]
    o_ref, m_ref, l_ref, acc_ref, carry_ref = refs[3 * g:]
    step = pl.program_id(1)
    qb = qbd_ref[0]
    cq = cq_ref[0]
    n_new = knew_ref.shape[1]

    @pl.when(step == 0)
    def _():
        s = _dot_nt(qb, knew_ref[0]) + cq[:, 0:n_new] - cnew_ref[0]
        t_row = lax.broadcasted_iota(jnp.int32, s.shape, 0) // n_heads
        j = lax.broadcasted_iota(jnp.int32, s.shape, 1)
        s = jnp.where((j <= t_row) & (j < seq), s, NEG)
        m = jnp.max(s, axis=1, keepdims=True)
        p = jnp.exp(s - m)
        m_ref[...] = m
        l_ref[...] = jnp.sum(p, axis=1, keepdims=True)
        acc_ref[...] = _dot(p.astype(BF16), vnew_ref[0])
        carry_ref[...] = jnp.zeros_like(carry_ref)

    carry = carry_ref[...]
    s_parts = []
    for p_i in range(g):
        sfx = lf_refs[p_i][0]
        suffix = sfx[:, 0:LANES] + carry
        carry = carry + sfx[:, LANES:2 * LANES]
        bias = jnp.concatenate([suffix] * seq, axis=0) + cq
        s_parts.append(_dot(qb, kt_refs[p_i][0].astype(BF16)) + bias)
    carry_ref[...] = carry
    s = jnp.concatenate(s_parts, axis=1)
    m_prev = m_ref[...]
    m_new = jnp.maximum(m_prev, jnp.max(s, axis=1, keepdims=True))
    p = jnp.exp(s - m_new)
    alpha = jnp.exp(m_prev - m_new)
    l_ref[...] = alpha * l_ref[...] + jnp.sum(p, axis=1, keepdims=True)
    pb = p.astype(BF16)
    pv = _dot_nt(pb[:, 0:LANES], vt_refs[0][0].astype(BF16))
    for p_i in range(1, g):
        pv = pv + _dot_nt(pb[:, p_i * LANES:(p_i + 1) * LANES], vt_refs[p_i][0].astype(BF16))
    acc_ref[...] = alpha * acc_ref[...] + pv
    m_ref[...] = m_new

    @pl.when(step == pl.num_programs(1) - 1)
    def _():
        o = acc_ref[...] / l_ref[...]
        r = lax.broadcasted_iota(jnp.int32, o.shape, 0)
        col = lax.broadcasted_iota(jnp.int32, o.shape, 1)
        o = jnp.where(col // head_dim == r % n_heads, o, 0.0)
        o_ref[0] = jnp.sum(o.reshape(seq, n_heads, o.shape[1]), axis=1)


def _attn_sample(page_table, qbd, cq, knew, vnew, cnew, kt_pool, vt_pool, lf_pool, *, seq, n_heads, head_dim):
    dec_batch, n_pages = page_table.shape
    g = PAGES_PER_STEP
    rows, width = qbd.shape[1], qbd.shape[2]
    page = kt_pool.shape[2]

    def per_batch(a):
        return pl.BlockSpec((1,) + a.shape[1:], lambda b, s, pt: (b,) + (0,) * (a.ndim - 1))

    def paged(a, p_i):
        return pl.BlockSpec((1,) + a.shape[1:],
                            lambda b, s, pt: (pt[b * n_pages + n_pages - 1 - (s * g + p_i)], 0, 0))

    in_specs = [per_batch(a) for a in (qbd, cq, knew, vnew, cnew)]
    in_specs += [paged(kt_pool, p_i) for p_i in range(g)]
    in_specs += [paged(vt_pool, p_i) for p_i in range(g)]
    in_specs += [paged(lf_pool, p_i) for p_i in range(g)]
    return pl.pallas_call(
        functools.partial(_attn_sample_body, n_pages_step=g, seq=seq, n_heads=n_heads, head_dim=head_dim),
        out_shape=jax.ShapeDtypeStruct((dec_batch, seq, width), F32),
        grid_spec=pltpu.PrefetchScalarGridSpec(
            num_scalar_prefetch=1,
            grid=(dec_batch, n_pages // g),
            in_specs=in_specs,
            out_specs=pl.BlockSpec((1, seq, width), lambda b, s, pt: (b, 0, 0)),
            scratch_shapes=[pltpu.VMEM((rows, 1), F32), pltpu.VMEM((rows, 1), F32),
                            pltpu.VMEM((rows, width), F32), pltpu.VMEM((n_heads, LANES), F32)],
        ),
        compiler_params=pltpu.CompilerParams(dimension_semantics=("arbitrary", "arbitrary"),
                                             vmem_limit_bytes=VMEM_LIMIT),
        name="attn_sample",
    )(page_table.reshape(-1), qbd, cq, knew, vnew, cnew, *([kt_pool] * g), *([vt_pool] * g), *([lf_pool] * g))


def kernel(x_prompt, x_sample, cache_k, cache_v, cache_logf, state_conv, page_table, g_ffn1, w1_ffn1, w3_ffn1,
           w2_ffn1, g_mix, w_in, b_f, conv_w, g_attn_out, g_conv_out, w_out, g_ffn2, w1_ffn2, w3_ffn2, w2_ffn2,
           g_final):
    batch, seq, d_model = x_prompt.shape
    dec_batch, dec_seq, _ = x_sample.shape
    depth, n_pool, page, n_heads, head_dim = cache_k.shape
    width = n_heads * head_dim
    assert depth == 1 and seq % ROW_TILE == 0 and page == LANES and 2 * head_dim == LANES
    assert page_table.shape[1] % PAGES_PER_STEP == 0 and w_in.shape[2] == 6 * width + n_heads

    xp = x_prompt.reshape(batch * seq, d_model)
    xs = x_sample.reshape(dec_batch * dec_seq, d_model)
    row = lambda a: a.reshape(1, -1)
    l = 0

    bw = lambda a: a.astype(BF16)
    wi = w_in[l]
    wf = jnp.tile(wi[:, 3 * width:3 * width + n_heads], (1, 3))
    w_proj = bw(jnp.concatenate([wi[:, :3 * width], wi[:, 3 * width + n_heads:],
                                 jnp.pad(wf, ((0, 0), (0, LANES - 3 * n_heads)))], axis=1))
    bf = jnp.pad(jnp.tile(b_f[l], 3), (0, LANES - 3 * n_heads)).reshape(1, LANES)
    ffn1 = (row(g_ffn1[l]), bw(w1_ffn1[l]), bw(w3_ffn1[l]), bw(w2_ffn1[l]))
    ffn2 = (row(g_ffn2[l]), bw(w1_ffn2[l]), bw(w3_ffn2[l]), bw(w2_ffn2[l]))
    wo = bw(w_out[l])
    merge_w = (row(g_attn_out[l]), wo[:width], wo[width:])
    gmix, cw, gco, gfin = row(g_mix[l]), conv_w[l], row(g_conv_out[l]), row(g_final)

    xp1 = _ffn(xp, *ffn1)
    kt_p, vt_p, logft_p, qa, ka, va, cn_p, tail_p = _proj_prompt(
        xp1, gmix, w_proj, bf, cw, gco, batch=batch, n_heads=n_heads, head_dim=head_dim)
    o_p = _attn_prompt(qa, ka, va, head_dim=head_dim).reshape(batch * seq, width)
    y_p = _ffn(xp1, *ffn2, merge=(o_p, cn_p) + merge_w, g_final=gfin)

    xs1 = _ffn(xs, *ffn1)
    st = state_conv[l]
    s1 = jnp.repeat(st[:, 1:2], dec_seq, axis=1).reshape(dec_batch * dec_seq, width)
    s2 = jnp.pad(st, ((0, 0), (0, dec_seq - 2), (0, 0))).reshape(dec_batch * dec_seq, width)
    q_s, k_s, v_s, logf_s, c_s, uc_s, cn_s = _proj_sample(
        xs1, gmix, w_proj, bf, cw, gco, s1, s2, seq=dec_seq, n_heads=n_heads, head_dim=head_dim)

    eye = jnp.eye(n_heads, dtype=F32)
    q4 = q_s.reshape(dec_batch, dec_seq, n_heads, 1, head_dim)
    qbd = bw((q4 * eye[None, None, :, :, None]).reshape(dec_batch, dec_seq * n_heads, width))
    c4 = c_s[:, :n_heads].reshape(dec_batch, dec_seq, n_heads)
    cq = jnp.broadcast_to(c4.reshape(dec_batch, dec_seq * n_heads, 1), (dec_batch, dec_seq * n_heads, LANES))
    n_new = 16
    cnew = jnp.broadcast_to(jnp.transpose(c4, (0, 2, 1))[:, None], (dec_batch, dec_seq, n_heads, dec_seq))
    cnew = jnp.pad(cnew.reshape(dec_batch, dec_seq * n_heads, dec_seq), ((0, 0), (0, 0), (0, n_new - dec_seq)))
    pad_new = lambda a: bw(jnp.pad(a.reshape(dec_batch, dec_seq, width), ((0, 0), (0, n_new - dec_seq), (0, 0))))
    kt_pool = jnp.transpose(cache_k[l], (0, 2, 3, 1)).reshape(n_pool, width, page)
    vt_pool = jnp.transpose(cache_v[l], (0, 2, 3, 1)).reshape(n_pool, width, page)
    lf_pool = _suffix_pool(jnp.transpose(cache_logf[l], (0, 2, 1)))
    o_s = _attn_sample(page_table, qbd, cq, pad_new(k_s), pad_new(v_s), cnew, kt_pool, vt_pool, lf_pool,
                       seq=dec_seq, n_heads=n_heads, head_dim=head_dim).reshape(dec_batch * dec_seq, width)
    y_s = _ffn(xs1, *ffn2, merge=(o_s, cn_s) + merge_w, g_final=gfin)

    hd = (n_heads, head_dim)
    untranspose = lambda a: jnp.transpose(a.reshape(batch, n_heads, head_dim, seq), (0, 3, 1, 2))[None]
    return (
        y_p.reshape(batch, seq, d_model),
        y_s.reshape(dec_batch, dec_seq, d_model),
        untranspose(kt_p),
        untranspose(vt_p),
        jnp.transpose(logft_p, (0, 2, 1))[None],
        tail_p.reshape(1, batch, 2, width),
        k_s.reshape(1, dec_batch, dec_seq, *hd),
        v_s.reshape(1, dec_batch, dec_seq, *hd),
        logf_s.reshape(1, dec_batch, dec_seq, n_heads),
        uc_s.reshape(1, dec_batch, dec_seq, width)[:, :, dec_seq - 2:],
    )
```

```python
import functools
import math

import jax
import jax.numpy as jnp
from jax import lax
from jax.experimental import pallas as pl
from jax.experimental.pallas import tpu as pltpu

F32 = jnp.float32
BF16 = jnp.bfloat16
EPS = 1e-6
NEG = -1e30
LANES = 128
ROW_TILE = 512
PAGES_PER_STEP = 16
HEADS_PER_STEP = 8
SHIFT_SLACK = 64.0
SAMPLE_SLACK = 40.0
SUFFIX_ROWS = 2048
LOG2E = 1.4426950408889634
VMEM_LIMIT = 56 * 1024 * 1024
AUG = 128
V_ROWS = 80


def _dot(a, b):
    return jnp.dot(a, b, preferred_element_type=F32)


def _dot_nt(a, b):
    return lax.dot_general(a, b, (((1,), (1,)), ((), ())), preferred_element_type=F32)


def _rms(x, g):
    return x * lax.rsqrt(jnp.mean(x * x, axis=-1, keepdims=True) + EPS) * g


def _split3(x):
    hi = x.astype(BF16).astype(F32)
    r1 = x - hi
    mid = r1.astype(BF16).astype(F32)
    lo = (r1 - mid).astype(BF16).astype(F32)
    return hi, mid, lo


def _const_spec(a):
    nd = a.ndim
    return pl.BlockSpec(a.shape, lambda *_: (0,) * nd, pipeline_mode=pl.Buffered(1))


def _ffn_body(*refs, merge, final):
    x_ref, g_ref, w1_ref, w3_ref, w2_ref = refs[:5]
    rest = refs[5:]
    x = x_ref[...]
    if merge:
        o_ref, cn_ref, gao_ref, woa_ref, woc_ref = rest[:5]
        rest = rest[5:]
        an = _rms(o_ref[...], gao_ref[...]).astype(BF16)
        x = x + (_dot(an, woa_ref[...]) + _dot(cn_ref[...], woc_ref[...]))
    if final:
        gfin_ref = rest[0]
        rest = rest[1:]
    out_ref = rest[0]
    h = _rms(x, g_ref[...]).astype(BF16)
    a = _dot(h, w1_ref[...])
    b = _dot(h, w3_ref[...])
    act = (a * jax.nn.sigmoid(a) * b).astype(BF16)
    y = x + 0.5 * _dot(act, w2_ref[...])
    if final:
        y = _rms(y, gfin_ref[...])
    out_ref[...] = y


def _ffn(x, g, w1, w3, w2, *, merge=None, g_final=None):
    n, d = x.shape
    tm = min(ROW_TILE, n)
    row = lambda i: (i, 0)
    args = [x, g, w1, w3, w2]
    specs = [pl.BlockSpec((tm, d), row)] + [_const_spec(a) for a in args[1:]]
    if merge is not None:
        o, cn, gao, woa, woc = merge
        args += [o, cn, gao, woa, woc]
        specs += [pl.BlockSpec((tm, o.shape[1]), row), pl.BlockSpec((tm, cn.shape[1]), row),
                  _const_spec(gao), _const_spec(woa), _const_spec(woc)]
    if g_final is not None:
        args.append(g_final)
        specs.append(_const_spec(g_final))
    return pl.pallas_call(
        functools.partial(_ffn_body, merge=merge is not None, final=g_final is not None),
        out_shape=jax.ShapeDtypeStruct((n, d), F32),
        grid=(n // tm,),
        in_specs=specs,
        out_specs=pl.BlockSpec((tm, d), row),
        compiler_params=pltpu.CompilerParams(dimension_semantics=("arbitrary",), vmem_limit_bytes=VMEM_LIMIT),
        name="ffn_merge" if merge is not None else "ffn",
    )(*args)


def _proj_common(x_ref, g_ref, w_ref, bf_ref, tri_ref, width, scale):
    a = width
    h = _rms(x_ref[...], g_ref[...]).astype(BF16)
    p = _dot(h, w_ref[...])
    q = p[:, 0:a] * scale
    k = p[:, a:2 * a]
    v = p[:, 2 * a:3 * a]
    bg = p[:, 3 * a:4 * a]
    uc = p[:, 4 * a:5 * a] * p[:, 5 * a:6 * a]
    z = p[:, 6 * a:6 * a + LANES] + bf_ref[...]
    lane = lax.broadcasted_iota(jnp.int32, z.shape, 1)
    logf = jnp.minimum(z, 0.0) - jnp.log1p(jnp.exp(-jnp.abs(z)))
    l3 = jnp.where(lane < 24, logf, 0.0)
    hi, mid, lo = _split3(l3)
    packed = jnp.where(lane < 8, hi, jnp.where(lane < 16, mid, lo)).astype(BF16)
    r = _dot(tri_ref[...], packed)
    c = r + pltpu.roll(r, LANES - 8, axis=1) + pltpu.roll(r, LANES - 16, axis=1)
    c = jnp.where(lane < 8, c, 0.0)
    return q, k, v, bg, uc, logf, c, lane


def _conv_branch(uc, prev1, prev2, bg, cw_ref, gco_ref):
    y = cw_ref[0:1, :] * prev2 + cw_ref[1:2, :] * prev1 + cw_ref[2:3, :] * uc
    return _rms(bg * y, gco_ref[...]).astype(BF16)


def _proj_prompt_body(x_ref, g_ref, w_ref, bf_ref, tri_ref, cw_ref, gco_ref, psel_ref, asel_ref, hsel_ref,
                      kt_ref, vt_ref, logf_ref, qa_ref, ka_ref, va_ref, cn_ref, tail_ref,
                      carry_c, carry_u, *, n_heads, head_dim, scale):
    tm = x_ref.shape[0]
    width = n_heads * head_dim

    @pl.when(pl.program_id(1) == 0)
    def _():
        carry_c[...] = jnp.zeros_like(carry_c)
        carry_u[...] = jnp.zeros_like(carry_u)

    q, k, v, bg, uc, logf, c, lane = _proj_common(x_ref, g_ref, w_ref, bf_ref, tri_ref, width, scale * LOG2E)
    q_t = q.T
    v_t = v.T
    kt_ref[0] = k.T
    vt_ref[0] = v_t
    logf_ref[0] = logf.T[0:n_heads]

    c = c + carry_c[...]
    carry_c[...] = c[tm - 1:tm, :]
    c2 = c * LOG2E

    def packed_parts(x):
        rep = jnp.where(lane < 8, x, jnp.where(lane < 16, pltpu.roll(x, 8, axis=1),
                                               jnp.where(lane < 24, pltpu.roll(x, 16, axis=1), 0.0)))
        hi, mid, lo = _split3(rep)
        return jnp.where(lane < 8, hi, jnp.where(lane < 16, mid, jnp.where(lane < 24, lo,
                         jnp.where(lane == 24, 1.0, 0.0)))).astype(BF16)

    shift = _dot((q * k).astype(BF16), hsel_ref[...])
    k_extra = _dot(packed_parts(c2), psel_ref[...])
    q_extra = _dot_nt(asel_ref[...], packed_parts(c2 - shift))

    lane_a = lax.broadcasted_iota(jnp.int32, (tm, AUG), 1)
    row8 = lax.broadcasted_iota(jnp.int32, (8, tm), 0)
    ones_row = jnp.where(row8 == 0, 1.0, 0.0)
    pad_q = jnp.zeros((AUG - head_dim - 8, tm), F32)
    pad_v = jnp.zeros((V_ROWS - head_dim - 8, tm), F32)
    for h in range(n_heads):
        blk = k[:, (h // 2) * AUG:(h // 2 + 1) * AUG]
        if h % 2:
            blk = pltpu.roll(blk, head_dim, axis=1)
        ka_ref[0, h] = jnp.where(lane_a < head_dim, blk, k_extra[:, h * AUG:(h + 1) * AUG]).astype(BF16)
        qa_ref[0, h] = jnp.concatenate(
            [q_t[h * head_dim:(h + 1) * head_dim], q_extra[h * 8:(h + 1) * 8], pad_q], axis=0).astype(BF16)
        va_ref[0, h, 0] = jnp.concatenate(
            [v_t[h * head_dim:(h + 1) * head_dim], ones_row, pad_v], axis=0).astype(BF16)

    rows = lax.broadcasted_iota(jnp.int32, uc.shape, 0)
    last1 = carry_u[7:8, :]
    last2 = carry_u[6:7, :]
    prev1 = jnp.where(rows == 0, last1, pltpu.roll(uc, 1, axis=0))
    prev2 = jnp.where(rows == 0, last2, jnp.where(rows == 1, last1, pltpu.roll(uc, 2, axis=0)))
    cn_ref[...] = _conv_branch(uc, prev1, prev2, bg, cw_ref, gco_ref)
    tail_ref[0] = uc[tm - 2:tm, :]
    carry_u[...] = uc[tm - 8:tm, :]


def _proj_sample_body(x_ref, g_ref, w_ref, bf_ref, tri_ref, cw_ref, gco_ref, s1_ref, s2_ref,
                      q_ref, k_ref, v_ref, logf_ref, c_ref, uc_ref, cn_ref, *, n_heads, head_dim, scale, seq):
    q, k, v, bg, uc, logf, c, _ = _proj_common(x_ref, g_ref, w_ref, bf_ref, tri_ref, n_heads * head_dim, scale)
    q_ref[...] = q
    k_ref[...] = k
    v_ref[...] = v
    logf_ref[...] = logf[:, 0:n_heads]
    c_ref[...] = c
    uc_ref[...] = uc
    t = lax.broadcasted_iota(jnp.int32, uc.shape, 0) % seq
    prev1 = jnp.where(t == 0, s1_ref[...], pltpu.roll(uc, 1, axis=0))
    prev2 = jnp.where(t < 2, s2_ref[...], pltpu.roll(uc, 2, axis=0))
    cn_ref[...] = _conv_branch(uc, prev1, prev2, bg, cw_ref, gco_ref)


def _selectors(n_heads):
    psel = jnp.zeros((LANES, n_heads * AUG), F32)
    asel = jnp.zeros((n_heads * 8, LANES), F32)
    for h in range(n_heads):
        base = h * AUG + 64
        psel = psel.at[24, base:base + 3].set(1.0)
        for part in range(3):
            psel = psel.at[part * 8 + h, base + 3 + part].set(-1.0)
            asel = asel.at[h * 8 + part, part * 8 + h].set(1.0)
        asel = asel.at[h * 8 + 3:h * 8 + 6, 24].set(1.0)
    return psel.astype(BF16), asel.astype(BF16)


def _proj_prompt(x, g, w, bf, cw, gco, *, batch, n_heads, head_dim):
    n, d = x.shape
    seq = n // batch
    tm = ROW_TILE
    nb = seq // tm
    width = n_heads * head_dim
    tri = jnp.tril(jnp.ones((tm, tm), F32)).astype(BF16)
    psel, asel = _selectors(n_heads)
    hsel = (jnp.arange(width)[:, None] // head_dim == jnp.arange(LANES)[None, :]).astype(BF16)
    row = lambda b, i: (b * nb + i, 0)
    consts = [g, w, bf, tri, cw, gco, psel, asel, hsel]
    out_shape = (
        jax.ShapeDtypeStruct((batch, width, seq), F32),
        jax.ShapeDtypeStruct((batch, width, seq), F32),
        jax.ShapeDtypeStruct((batch, n_heads, seq), F32),
        jax.ShapeDtypeStruct((batch, n_heads, AUG, seq), BF16),
        jax.ShapeDtypeStruct((batch, n_heads, seq, AUG), BF16),
        jax.ShapeDtypeStruct((batch, n_heads, nb, V_ROWS, tm), BF16),
        jax.ShapeDtypeStruct((n, width), BF16),
        jax.ShapeDtypeStruct((batch, 2, width), F32),
    )
    out_specs = (
        pl.BlockSpec((1, width, tm), lambda b, i: (b, 0, i)),
        pl.BlockSpec((1, width, tm), lambda b, i: (b, 0, i)),
        pl.BlockSpec((1, n_heads, tm), lambda b, i: (b, 0, i)),
        pl.BlockSpec((1, n_heads, AUG, tm), lambda b, i: (b, 0, 0, i)),
        pl.BlockSpec((1, n_heads, tm, AUG), lambda b, i: (b, 0, i, 0)),
        pl.BlockSpec((1, n_heads, 1, V_ROWS, tm), lambda b, i: (b, 0, i, 0, 0)),
        pl.BlockSpec((tm, width), row),
        pl.BlockSpec((1, 2, width), lambda b, i: (b, 0, 0)),
    )
    return pl.pallas_call(
        functools.partial(_proj_prompt_body, n_heads=n_heads, head_dim=head_dim, scale=1.0 / math.sqrt(head_dim)),
        out_shape=out_shape,
        grid=(batch, nb),
        in_specs=[pl.BlockSpec((tm, d), row)] + [_const_spec(a) for a in consts],
        out_specs=out_specs,
        scratch_shapes=[pltpu.VMEM((1, LANES), F32), pltpu.VMEM((8, width), F32)],
        compiler_params=pltpu.CompilerParams(dimension_semantics=("arbitrary", "arbitrary"),
                                             vmem_limit_bytes=VMEM_LIMIT),
        name="proj_prompt",
    )(x, *consts)


def _proj_sample(x, g, w, bf, cw, gco, s1, s2, *, seq, n_heads, head_dim):
    n, d = x.shape
    width = n_heads * head_dim
    r = jnp.arange(n)
    tri = ((r[None, :] <= r[:, None]) & (r[None, :] // seq == r[:, None] // seq)).astype(BF16)
    args = [x, g, w, bf, tri, cw, gco, s1, s2]
    shapes = [(n, width)] * 3 + [(n, n_heads), (n, LANES), (n, width)]
    out_shape = tuple(jax.ShapeDtypeStruct(s, F32) for s in shapes) + (jax.ShapeDtypeStruct((n, width), BF16),)
    return pl.pallas_call(
        functools.partial(_proj_sample_body, n_heads=n_heads, head_dim=head_dim,
                          scale=1.0 / math.sqrt(head_dim), seq=seq),
        out_shape=out_shape,
        compiler_params=pltpu.CompilerParams(vmem_limit_bytes=VMEM_LIMIT),
        name="proj_sample",
    )(*args)


def _attn_prompt_body(qa_ref, ka_ref, va_ref, o_ref, acc_ref, s_ref, ot_ref, *, head_dim):
    i = pl.program_id(2)
    n_h = qa_ref.shape[1]
    tq = qa_ref.shape[3]
    tk = tq
    sub = lax.broadcasted_iota(jnp.int32, (tk, tq), 0)
    lan = lax.broadcasted_iota(jnp.int32, (tk, tq), 1)
    causal = sub <= lan

    def scores(j, hh):
        k_j = ka_ref[0, hh, pl.ds(pl.multiple_of(j * tk, tk), tk), :]
        return _dot(k_j, qa_ref[0, hh])

    def normalised(acc):
        return acc[0:head_dim] / acc[head_dim:head_dim + 1]

    top = []
    s_next = scores(i, 0)
    for hh in range(n_h):
        s = jnp.where(causal, s_next, NEG)
        s_next = scores(i, hh + 1) if hh + 1 < n_h else scores(0, 0)
        top.append(jnp.max(s, axis=0, keepdims=True))
        acc_ref[hh] = _dot(va_ref[0, hh, i], jnp.exp2(s).astype(BF16))
    s_ref[...] = s_next

    def one_pass_block(j, top):
        s_next = s_ref[...]
        new_top = []
        for hh in range(n_h):
            s = s_next
            s_next = scores(j, hh + 1) if hh + 1 < n_h else scores(j + 1, 0)
            new_top.append(jnp.maximum(top[hh], jnp.max(s, axis=0, keepdims=True)))
            acc_ref[hh] += _dot(va_ref[0, hh, j], jnp.exp2(s).astype(BF16))
        s_ref[...] = s_next
        return tuple(new_top)

    top = lax.fori_loop(0, i, one_pass_block, tuple(top))
    o_ref[0] = jnp.concatenate([normalised(acc_ref[hh]) for hh in range(n_h)], axis=0).T
    hi, lo = top[0], top[0]
    for t in top[1:]:
        hi, lo = jnp.maximum(hi, t), jnp.minimum(lo, t)

    @pl.when((jnp.max(hi) > SHIFT_SLACK) | (jnp.min(lo) < -SHIFT_SLACK))
    def _():
        def online_head(hh, _):
            q_t = qa_ref[0, hh]

            def sc(j):
                return _dot(ka_ref[0, hh, pl.ds(pl.multiple_of(j * tk, tk), tk), :], q_t)

            s = jnp.where(causal, sc(i), NEG)
            m = jnp.max(s, axis=0, keepdims=True)
            acc = _dot(va_ref[0, hh, i], jnp.exp2(s - m).astype(BF16))

            def body(j, carry):
                m, acc = carry
                s = sc(j)
                m_new = jnp.maximum(m, jnp.max(s, axis=0, keepdims=True))
                p = jnp.exp2(s - m_new).astype(BF16)
                return m_new, jnp.exp2(m - m_new) * acc + _dot(va_ref[0, hh, j], p)

            m, acc = lax.fori_loop(0, i, body, (m, acc))
            ot_ref[pl.ds(pl.multiple_of(hh * head_dim, head_dim), head_dim), :] = normalised(acc)
            return 0

        lax.fori_loop(0, n_h, online_head, 0)
        o_ref[0] = ot_ref[...].T


def _attn_prompt(qa, ka, va, *, head_dim):
    batch, n_heads, _, seq = qa.shape
    nb, tq = va.shape[2], va.shape[4]
    hps = HEADS_PER_STEP
    resident = dict(pipeline_mode=pl.Buffered(1))
    return pl.pallas_call(
        functools.partial(_attn_prompt_body, head_dim=head_dim),
        out_shape=jax.ShapeDtypeStruct((batch, seq, n_heads * head_dim), F32),
        grid=(batch, n_heads // hps, nb),
        in_specs=[
            pl.BlockSpec((1, hps, AUG, tq), lambda b, g, i: (b, g, 0, i)),
            pl.BlockSpec((1, hps, seq, AUG), lambda b, g, i: (b, g, 0, 0), **resident),
            pl.BlockSpec((1, hps, nb, V_ROWS, tq), lambda b, g, i: (b, g, 0, 0, 0), **resident),
        ],
        out_specs=pl.BlockSpec((1, tq, hps * head_dim), lambda b, g, i: (b, i, g)),
        scratch_shapes=[pltpu.VMEM((hps, V_ROWS, tq), F32), pltpu.VMEM((tq, tq), F32),
                        pltpu.VMEM((hps * head_dim, tq), F32)],
        compiler_params=pltpu.CompilerParams(dimension_semantics=("arbitrary",) * 3, vmem_limit_bytes=VMEM_LIMIT),
        name="attn_prompt",
    )(qa, ka, va)


def _suffix_body(x_ref, rhs_ref, o_ref):
    hi, mid, lo = _split3(x_ref[...])
    rhs = rhs_ref[...]
    o_ref[...] = _dot(hi.astype(BF16), rhs) + _dot(mid.astype(BF16), rhs) + _dot(lo.astype(BF16), rhs)


def _suffix_pool(lf_pool):
    n_pool, n_heads, page = lf_pool.shape
    rows = n_pool * n_heads
    tr = SUFFIX_ROWS
    assert rows % tr == 0
    r = jnp.arange(page)
    rhs = jnp.concatenate([(r[:, None] > r[None, :]).astype(BF16), jnp.ones((page, page), BF16)], axis=1)
    out = pl.pallas_call(
        _suffix_body,
        out_shape=jax.ShapeDtypeStruct((rows, 2 * page), F32),
        grid=(rows // tr,),
        in_specs=[pl.BlockSpec((tr, page), lambda i: (i, 0)), _const_spec(rhs)],
        out_specs=pl.BlockSpec((tr, 2 * page), lambda i: (i, 0)),
        compiler_params=pltpu.CompilerParams(dimension_semantics=("arbitrary",), vmem_limit_bytes=VMEM_LIMIT),
        name="suffix_pool",
    )(lf_pool.reshape(rows, page), rhs)
    return out.reshape(n_pool, n_heads, 2 * page)


def _attn_sample_body(pt_ref, qbd_ref, cq_ref, knew_ref, vnew_ref, cnew_ref, kt_hbm, vt_hbm, sfx_hbm, o_ref,
                      kbuf, vbuf, sbuf, sem, m_ref, l_ref, lsum_ref, top_ref, acc_ref, carry_ref,
                      *, g, n_pages, seq, n_heads, head_dim):
    dec_batch = qbd_ref.shape[0]
    n_new = knew_ref.shape[1]
    per_row = n_pages // g
    n_chunks = dec_batch * per_row

    def page_copies(c, slot, p_i):
        page = pt_ref[(c // per_row) * n_pages + (n_pages - 1) - (c % per_row) * g - p_i]
        return (pltpu.make_async_copy(kt_hbm.at[page], kbuf.at[slot, p_i], sem.at[0, slot]),
                pltpu.make_async_copy(vt_hbm.at[page], vbuf.at[slot, p_i], sem.at[1, slot]),
                pltpu.make_async_copy(sfx_hbm.at[page], sbuf.at[slot, p_i], sem.at[2, slot]))

    def start_chunk(c, slot):
        def body(p_i, _):
            for cp in page_copies(c, slot, p_i):
                cp.start()
            return 0
        lax.fori_loop(0, g, body, 0)

    def wait_chunk(c, slot):
        def body(p_i, _):
            for cp in page_copies(c, slot, p_i):
                cp.wait()
            return 0
        lax.fori_loop(0, g, body, 0)

    def start_row(b):
        qb, cq = qbd_ref[b], cq_ref[b]
        s = _dot_nt(qb, knew_ref[b]) + cq[:, 0:n_new] - cnew_ref[b]
        t_row = lax.broadcasted_iota(jnp.int32, s.shape, 0) // n_heads
        j = lax.broadcasted_iota(jnp.int32, s.shape, 1)
        s = jnp.where((j <= t_row) & (j < seq), s, NEG)
        m = jnp.max(s, axis=1, keepdims=True)
        p = jnp.exp(s - m)
        m_ref[...] = m
        l_ref[...] = jnp.sum(p, axis=1, keepdims=True)
        acc_ref[...] = _dot(p.astype(BF16), vnew_ref[b])
        lsum_ref[...] = jnp.zeros_like(lsum_ref)
        top_ref[...] = jnp.full_like(top_ref, NEG)
        carry_ref[...] = jnp.zeros_like(carry_ref)

    def page_scores(b, slot):
        qb, cq = qbd_ref[b], cq_ref[b]
        carry = carry_ref[...]
        parts = []
        for p_i in range(g):
            sfx = sbuf[slot, p_i]
            suffix = sfx[:, 0:LANES] + carry
            carry = carry + sfx[:, LANES:2 * LANES]
            bias = jnp.concatenate([suffix] * seq, axis=0) + cq
            parts.append(_dot(qb, kbuf[slot, p_i].astype(BF16)) + bias)
        carry_ref[...] = carry
        return parts

    def one_pass_chunk(b, slot):
        m = m_ref[...]
        top, lsum, acc = top_ref[...], lsum_ref[...], acc_ref[...]
        for p_i, s in enumerate(page_scores(b, slot)):
            top = jnp.maximum(top, s)
            p = jnp.exp(s - m)
            lsum = lsum + p
            acc = acc + _dot_nt(p.astype(BF16), vbuf[slot, p_i].astype(BF16))
        top_ref[...], lsum_ref[...], acc_ref[...] = top, lsum, acc

    def online_chunk(b, slot):
        s = jnp.concatenate(page_scores(b, slot), axis=1)
        m_prev = m_ref[...]
        m_new = jnp.maximum(m_prev, jnp.max(s, axis=1, keepdims=True))
        p = jnp.exp(s - m_new)
        alpha = jnp.exp(m_prev - m_new)
        l_ref[...] = alpha * l_ref[...] + jnp.sum(p, axis=1, keepdims=True)
        pb = p.astype(BF16)
        acc = alpha * acc_ref[...]
        for p_i in range(g):
            acc = acc + _dot_nt(pb[:, p_i * LANES:(p_i + 1) * LANES], vbuf[slot, p_i].astype(BF16))
        acc_ref[...] = acc
        m_ref[...] = m_new

    def write_row(b, l):
        o = acc_ref[...] / l
        r = lax.broadcasted_iota(jnp.int32, o.shape, 0)
        col = lax.broadcasted_iota(jnp.int32, o.shape, 1)
        o = jnp.where(col // head_dim == r % n_heads, o, 0.0)
        o_ref[b] = jnp.sum(o.reshape(seq, n_heads, o.shape[1]), axis=1)

    def finish_row(b, slot):
        write_row(b, l_ref[...] + jnp.sum(lsum_ref[...], axis=1, keepdims=True))
        excess = jnp.max(top_ref[...], axis=1, keepdims=True) - m_ref[...]

        @pl.when(jnp.max(excess) > SAMPLE_SLACK)
        def _():
            start_row(b)

            def redo(r, _):
                start_chunk(b * per_row + r, slot)
                wait_chunk(b * per_row + r, slot)
                online_chunk(b, slot)
                return 0

            lax.fori_loop(0, per_row, redo, 0)
            write_row(b, l_ref[...])

    t = pl.program_id(0)

    @pl.when(t == 0)
    def _():
        start_chunk(0, 0)

    for slot in (0, 1):
        c = 2 * t + slot
        b, r = c // per_row, c % per_row

        @pl.when(c + 1 < n_chunks)
        def _():
            start_chunk(c + 1, 1 - slot)

        @pl.when(r == 0)
        def _():
            start_row(b)

        wait_chunk(c, slot)
        one_pass_chunk(b, slot)

        if slot == 1:
            @pl.when(r == per_row - 1)
            def _():
                finish_row(b, slot)


def _attn_sample(page_table, qbd, cq, knew, vnew, cnew, kt_pool, vt_pool, sfx_pool, *, seq, n_heads, head_dim):
    dec_batch, n_pages = page_table.shape
    g = PAGES_PER_STEP
    rows, width = qbd.shape[1], qbd.shape[2]
    page = kt_pool.shape[2]
    assert n_pages % (2 * g) == 0
    vmem = pl.BlockSpec(memory_space=pltpu.VMEM)
    hbm = pl.BlockSpec(memory_space=pl.ANY)
    return pl.pallas_call(
        functools.partial(_attn_sample_body, g=g, n_pages=n_pages, seq=seq, n_heads=n_heads, head_dim=head_dim),
        out_shape=jax.ShapeDtypeStruct((dec_batch, seq, width), F32),
        grid=(dec_batch * n_pages // (2 * g),),
        in_specs=[pl.BlockSpec(memory_space=pltpu.SMEM)] + [vmem] * 5 + [hbm] * 3,
        out_specs=vmem,
        scratch_shapes=[
            pltpu.VMEM((2, g, width, page), F32), pltpu.VMEM((2, g, width, page), F32),
            pltpu.VMEM((2, g, n_heads, 2 * page), F32), pltpu.SemaphoreType.DMA((3, 2)),
            pltpu.VMEM((rows, 1), F32), pltpu.VMEM((rows, 1), F32), pltpu.VMEM((rows, LANES), F32),
            pltpu.VMEM((rows, LANES), F32), pltpu.VMEM((rows, width), F32), pltpu.VMEM((n_heads, LANES), F32),
        ],
        compiler_params=pltpu.CompilerParams(dimension_semantics=("arbitrary",), vmem_limit_bytes=VMEM_LIMIT),
        name="attn_sample",
    )(page_table.reshape(-1), qbd, cq, knew, vnew, cnew, kt_pool, vt_pool, sfx_pool)


def kernel(x_prompt, x_sample, cache_k, cache_v, cache_logf, state_conv, page_table, g_ffn1, w1_ffn1, w3_ffn1,
           w2_ffn1, g_mix, w_in, b_f, conv_w, g_attn_out, g_conv_out, w_out, g_ffn2, w1_ffn2, w3_ffn2, w2_ffn2,
           g_final):
    batch, seq, d_model = x_prompt.shape
    dec_batch, dec_seq, _ = x_sample.shape
    depth, n_pool, page, n_heads, head_dim = cache_k.shape
    width = n_heads * head_dim
    assert depth == 1 and seq % ROW_TILE == 0 and page == LANES and 2 * head_dim == LANES
    assert w_in.shape[2] == 6 * width + n_heads

    xp = x_prompt.reshape(batch * seq, d_model)
    xs = x_sample.reshape(dec_batch * dec_seq, d_model)
    row = lambda a: a.reshape(1, -1)
    l = 0

    bw = lambda a: a.astype(BF16)
    wi = w_in[l]
    wf = jnp.tile(wi[:, 3 * width:3 * width + n_heads], (1, 3))
    w_proj = bw(jnp.concatenate([wi[:, :3 * width], wi[:, 3 * width + n_heads:],
                                 jnp.pad(wf, ((0, 0), (0, LANES - 3 * n_heads)))], axis=1))
    bf = jnp.pad(jnp.tile(b_f[l], 3), (0, LANES - 3 * n_heads)).reshape(1, LANES)
    ffn1 = (row(g_ffn1[l]), bw(w1_ffn1[l]), bw(w3_ffn1[l]), bw(w2_ffn1[l]))
    ffn2 = (row(g_ffn2[l]), bw(w1_ffn2[l]), bw(w3_ffn2[l]), bw(w2_ffn2[l]))
    wo = bw(w_out[l])
    merge_w = (row(g_attn_out[l]), wo[:width], wo[width:])
    gmix, cw, gco, gfin = row(g_mix[l]), conv_w[l], row(g_conv_out[l]), row(g_final)

    xp1 = _ffn(xp, *ffn1)
    kt_p, vt_p, logft_p, qa, ka, va, cn_p, tail_p = _proj_prompt(
        xp1, gmix, w_proj, bf, cw, gco, batch=batch, n_heads=n_heads, head_dim=head_dim)
    o_p = _attn_prompt(qa, ka, va, head_dim=head_dim).reshape(batch * seq, width)
    y_p = _ffn(xp1, *ffn2, merge=(o_p, cn_p) + merge_w, g_final=gfin)

    xs1 = _ffn(xs, *ffn1)
    st = state_conv[l]
    s1 = jnp.repeat(st[:, 1:2], dec_seq, axis=1).reshape(dec_batch * dec_seq, width)
    s2 = jnp.pad(st, ((0, 0), (0, dec_seq - 2), (0, 0))).reshape(dec_batch * dec_seq, width)
    q_s, k_s, v_s, logf_s, c_s, uc_s, cn_s = _proj_sample(
        xs1, gmix, w_proj, bf, cw, gco, s1, s2, seq=dec_seq, n_heads=n_heads, head_dim=head_dim)

    eye = jnp.eye(n_heads, dtype=F32)
    q4 = q_s.reshape(dec_batch, dec_seq, n_heads, 1, head_dim)
    qbd = bw((q4 * eye[None, None, :, :, None]).reshape(dec_batch, dec_seq * n_heads, width))
    c4 = c_s[:, :n_heads].reshape(dec_batch, dec_seq, n_heads)
    cq = jnp.broadcast_to(c4.reshape(dec_batch, dec_seq * n_heads, 1), (dec_batch, dec_seq * n_heads, LANES))
    n_new = 16
    cnew = jnp.broadcast_to(jnp.transpose(c4, (0, 2, 1))[:, None], (dec_batch, dec_seq, n_heads, dec_seq))
    cnew = jnp.pad(cnew.reshape(dec_batch, dec_seq * n_heads, dec_seq), ((0, 0), (0, 0), (0, n_new - dec_seq)))
    pad_new = lambda a: bw(jnp.pad(a.reshape(dec_batch, dec_seq, width), ((0, 0), (0, n_new - dec_seq), (0, 0))))
    kt_pool = jnp.transpose(cache_k[l], (0, 2, 3, 1)).reshape(n_pool, width, page)
    vt_pool = jnp.transpose(cache_v[l], (0, 2, 3, 1)).reshape(n_pool, width, page)
    lf_pool = _suffix_pool(jnp.transpose(cache_logf[l], (0, 2, 1)))
    o_s = _attn_sample(page_table, qbd, cq, pad_new(k_s), pad_new(v_s), cnew, kt_pool, vt_pool, lf_pool,
                       seq=dec_seq, n_heads=n_heads, head_dim=head_dim).reshape(dec_batch * dec_seq, width)
    y_s = _ffn(xs1, *ffn2, merge=(o_s, cn_s) + merge_w, g_final=gfin)

    hd = (n_heads, head_dim)
    untranspose = lambda a: jnp.transpose(a.reshape(batch, n_heads, head_dim, seq), (0, 3, 1, 2))[None]
    return (
        y_p.reshape(batch, seq, d_model),
        y_s.reshape(dec_batch, dec_seq, d_model),
        untranspose(kt_p),
        untranspose(vt_p),
        jnp.transpose(logft_p, (0, 2, 1))[None],
        tail_p.reshape(1, batch, 2, width),
        k_s.reshape(1, dec_batch, dec_seq, *hd),
        v_s.reshape(1, dec_batch, dec_seq, *hd),
        logf_s.reshape(1, dec_batch, dec_seq, n_heads),
        uc_s.reshape(1, dec_batch, dec_seq, width)[:, :, dec_seq - 2:],
    )
```

```python
import functools
import math

import jax
import jax.numpy as jnp
from jax import lax
from jax.experimental import pallas as pl
from jax.experimental.pallas import tpu as pltpu

F32 = jnp.float32
BF16 = jnp.bfloat16
EPS = 1e-6
NEG = -1e30
LANES = 128
ROW_TILE = 512
SAMPLE_ROW_TILE = 256
PAGES_PER_STEP = 16
HEADS_PER_STEP = 8
SHIFT_SLACK = 64.0
SAMPLE_SLACK = 40.0
SUFFIX_ROWS = 2048
LOG2E = 1.4426950408889634
VMEM_LIMIT = 56 * 1024 * 1024
AUG = 128
V_ROWS = 80


def _dot(a, b):
    return jnp.dot(a, b, preferred_element_type=F32)


def _dot_nt(a, b):
    return lax.dot_general(a, b, (((1,), (1,)), ((), ())), preferred_element_type=F32)


def _rms(x, g):
    return x * lax.rsqrt(jnp.mean(x * x, axis=-1, keepdims=True) + EPS) * g


def _split3(x):
    hi = x.astype(BF16).astype(F32)
    r1 = x - hi
    mid = r1.astype(BF16).astype(F32)
    lo = (r1 - mid).astype(BF16).astype(F32)
    return hi, mid, lo


def _const_spec(a):
    nd = a.ndim
    return pl.BlockSpec(a.shape, lambda *_: (0,) * nd, pipeline_mode=pl.Buffered(1))


def _ffn_body(*refs, merge, final, sample):
    x_ref, g_ref, w1_ref, w3_ref, w2_ref = refs[:5]
    rest = refs[5:]
    if merge:
        o_ref, cn_ref, gao_ref, woa_ref, woc_ref = rest[:5]
        rest = rest[5:]
    if final:
        gfin_ref = rest[0]
        rest = rest[1:]
    if sample:
        sample_in, rest = rest[:_N_SAMPLE_IN], rest[_N_SAMPLE_IN:]
        out_ref, os_ref = rest[:2]
        _sample_chunk_pair(pl.program_id(0), *sample_in, os_ref, *rest[2:], **sample)
    else:
        out_ref = rest[0]
    x = x_ref[...]
    if merge:
        an = _rms(o_ref[...], gao_ref[...]).astype(BF16)
        x = x + (_dot(an, woa_ref[...]) + _dot(cn_ref[...], woc_ref[...]))
    h = _rms(x, g_ref[...]).astype(BF16)
    a = _dot(h, w1_ref[...])
    b = _dot(h, w3_ref[...])
    act = (a * jax.nn.sigmoid(a) * b).astype(BF16)
    y = x + 0.5 * _dot(act, w2_ref[...])
    if final:
        y = _rms(y, gfin_ref[...])
    out_ref[...] = y


def _ffn(x, g, w1, w3, w2, *, merge=None, g_final=None, sample=None):
    n, d = x.shape
    tm = min(ROW_TILE if sample is None else SAMPLE_ROW_TILE, n)
    row = lambda i: (i, 0)
    args = [x, g, w1, w3, w2]
    specs = [pl.BlockSpec((tm, d), row)] + [_const_spec(a) for a in args[1:]]
    if merge is not None:
        o, cn, gao, woa, woc = merge
        args += [o, cn, gao, woa, woc]
        specs += [pl.BlockSpec((tm, o.shape[1]), row), pl.BlockSpec((tm, cn.shape[1]), row),
                  _const_spec(gao), _const_spec(woa), _const_spec(woc)]
    if g_final is not None:
        args.append(g_final)
        specs.append(_const_spec(g_final))
    out_shape = jax.ShapeDtypeStruct((n, d), F32)
    out_specs = pl.BlockSpec((tm, d), row)
    scratch, static = [], None
    if sample is not None:
        operands, static = sample
        s_specs, s_out_shape, scratch, n_steps = _sample_specs(*operands, **static)
        assert n // tm == n_steps and len(operands) == _N_SAMPLE_IN
        args += list(operands)
        specs += s_specs
        out_shape = (out_shape, s_out_shape)
        out_specs = (out_specs, pl.BlockSpec(memory_space=pltpu.VMEM))
    return pl.pallas_call(
        functools.partial(_ffn_body, merge=merge is not None, final=g_final is not None, sample=static),
        out_shape=out_shape,
        grid=(n // tm,),
        in_specs=specs,
        out_specs=out_specs,
        scratch_shapes=scratch,
        compiler_params=pltpu.CompilerParams(dimension_semantics=("arbitrary",), vmem_limit_bytes=VMEM_LIMIT),
        name=("ffn_merge" if merge is not None else "ffn") + ("_sample" if sample is not None else ""),
    )(*args)


def _proj_common(x_ref, g_ref, w_ref, bf_ref, tri_ref, width, scale):
    a = width
    h = _rms(x_ref[...], g_ref[...]).astype(BF16)
    p = _dot(h, w_ref[...])
    q = p[:, 0:a] * scale
    k = p[:, a:2 * a]
    v = p[:, 2 * a:3 * a]
    bg = p[:, 3 * a:4 * a]
    uc = p[:, 4 * a:5 * a] * p[:, 5 * a:6 * a]
    z = p[:, 6 * a:6 * a + LANES] + bf_ref[...]
    lane = lax.broadcasted_iota(jnp.int32, z.shape, 1)
    logf = jnp.minimum(z, 0.0) - jnp.log1p(jnp.exp(-jnp.abs(z)))
    l3 = jnp.where(lane < 24, logf, 0.0)
    hi, mid, lo = _split3(l3)
    packed = jnp.where(lane < 8, hi, jnp.where(lane < 16, mid, lo)).astype(BF16)
    r = _dot(tri_ref[...], packed)
    c = r + pltpu.roll(r, LANES - 8, axis=1) + pltpu.roll(r, LANES - 16, axis=1)
    c = jnp.where(lane < 8, c, 0.0)
    return q, k, v, bg, uc, logf, c, lane


def _conv_branch(uc, prev1, prev2, bg, cw_ref, gco_ref):
    y = cw_ref[0:1, :] * prev2 + cw_ref[1:2, :] * prev1 + cw_ref[2:3, :] * uc
    return _rms(bg * y, gco_ref[...]).astype(BF16)


def _proj_prompt_body(x_ref, g_ref, w_ref, bf_ref, tri_ref, cw_ref, gco_ref, psel_ref, asel_ref, hsel_ref,
                      kt_ref, vt_ref, logf_ref, qa_ref, ka_ref, va_ref, cn_ref, tail_ref,
                      carry_c, carry_u, *, n_heads, head_dim, scale):
    tm = x_ref.shape[0]
    width = n_heads * head_dim

    @pl.when(pl.program_id(1) == 0)
    def _():
        carry_c[...] = jnp.zeros_like(carry_c)
        carry_u[...] = jnp.zeros_like(carry_u)

    q, k, v, bg, uc, logf, c, lane = _proj_common(x_ref, g_ref, w_ref, bf_ref, tri_ref, width, scale * LOG2E)
    q_t = q.T
    v_t = v.T
    kt_ref[0] = k.T
    vt_ref[0] = v_t
    logf_ref[0] = logf.T[0:n_heads]

    c = c + carry_c[...]
    carry_c[...] = c[tm - 1:tm, :]
    c2 = c * LOG2E

    def packed_parts(x):
        rep = jnp.where(lane < 8, x, jnp.where(lane < 16, pltpu.roll(x, 8, axis=1),
                                               jnp.where(lane < 24, pltpu.roll(x, 16, axis=1), 0.0)))
        hi, mid, lo = _split3(rep)
        return jnp.where(lane < 8, hi, jnp.where(lane < 16, mid, jnp.where(lane < 24, lo,
                         jnp.where(lane == 24, 1.0, 0.0)))).astype(BF16)

    shift = _dot((q * k).astype(BF16), hsel_ref[...])
    k_extra = _dot(packed_parts(c2), psel_ref[...])
    q_extra = _dot_nt(asel_ref[...], packed_parts(c2 - shift))

    lane_a = lax.broadcasted_iota(jnp.int32, (tm, AUG), 1)
    row8 = lax.broadcasted_iota(jnp.int32, (8, tm), 0)
    ones_row = jnp.where(row8 == 0, 1.0, 0.0)
    pad_q = jnp.zeros((AUG - head_dim - 8, tm), F32)
    pad_v = jnp.zeros((V_ROWS - head_dim - 8, tm), F32)
    for h in range(n_heads):
        blk = k[:, (h // 2) * AUG:(h // 2 + 1) * AUG]
        if h % 2:
            blk = pltpu.roll(blk, head_dim, axis=1)
        ka_ref[0, h] = jnp.where(lane_a < head_dim, blk, k_extra[:, h * AUG:(h + 1) * AUG]).astype(BF16)
        qa_ref[0, h] = jnp.concatenate(
            [q_t[h * head_dim:(h + 1) * head_dim], q_extra[h * 8:(h + 1) * 8], pad_q], axis=0).astype(BF16)
        va_ref[0, h, 0] = jnp.concatenate(
            [v_t[h * head_dim:(h + 1) * head_dim], ones_row, pad_v], axis=0).astype(BF16)

    rows = lax.broadcasted_iota(jnp.int32, uc.shape, 0)
    last1 = carry_u[7:8, :]
    last2 = carry_u[6:7, :]
    prev1 = jnp.where(rows == 0, last1, pltpu.roll(uc, 1, axis=0))
    prev2 = jnp.where(rows == 0, last2, jnp.where(rows == 1, last1, pltpu.roll(uc, 2, axis=0)))
    cn_ref[...] = _conv_branch(uc, prev1, prev2, bg, cw_ref, gco_ref)
    tail_ref[0] = uc[tm - 2:tm, :]
    carry_u[...] = uc[tm - 8:tm, :]


def _proj_sample_body(x_ref, g_ref, w_ref, bf_ref, tri_ref, cw_ref, gco_ref, s1_ref, s2_ref,
                      q_ref, k_ref, v_ref, logf_ref, c_ref, uc_ref, cn_ref, *, n_heads, head_dim, scale, seq):
    q, k, v, bg, uc, logf, c, _ = _proj_common(x_ref, g_ref, w_ref, bf_ref, tri_ref, n_heads * head_dim, scale)
    q_ref[...] = q
    k_ref[...] = k
    v_ref[...] = v
    logf_ref[...] = logf[:, 0:n_heads]
    c_ref[...] = c
    uc_ref[...] = uc
    t = lax.broadcasted_iota(jnp.int32, uc.shape, 0) % seq
    prev1 = jnp.where(t == 0, s1_ref[...], pltpu.roll(uc, 1, axis=0))
    prev2 = jnp.where(t < 2, s2_ref[...], pltpu.roll(uc, 2, axis=0))
    cn_ref[...] = _conv_branch(uc, prev1, prev2, bg, cw_ref, gco_ref)


def _selectors(n_heads):
    psel = jnp.zeros((LANES, n_heads * AUG), F32)
    asel = jnp.zeros((n_heads * 8, LANES), F32)
    for h in range(n_heads):
        base = h * AUG + 64
        psel = psel.at[24, base:base + 3].set(1.0)
        for part in range(3):
            psel = psel.at[part * 8 + h, base + 3 + part].set(-1.0)
            asel = asel.at[h * 8 + part, part * 8 + h].set(1.0)
        asel = asel.at[h * 8 + 3:h * 8 + 6, 24].set(1.0)
    return psel.astype(BF16), asel.astype(BF16)


def _proj_prompt(x, g, w, bf, cw, gco, *, batch, n_heads, head_dim):
    n, d = x.shape
    seq = n // batch
    tm = ROW_TILE
    nb = seq // tm
    width = n_heads * head_dim
    tri = jnp.tril(jnp.ones((tm, tm), F32)).astype(BF16)
    psel, asel = _selectors(n_heads)
    hsel = (jnp.arange(width)[:, None] // head_dim == jnp.arange(LANES)[None, :]).astype(BF16)
    row = lambda b, i: (b * nb + i, 0)
    consts = [g, w, bf, tri, cw, gco, psel, asel, hsel]
    out_shape = (
        jax.ShapeDtypeStruct((batch, width, seq), F32),
        jax.ShapeDtypeStruct((batch, width, seq), F32),
        jax.ShapeDtypeStruct((batch, n_heads, seq), F32),
        jax.ShapeDtypeStruct((batch, n_heads, AUG, seq), BF16),
        jax.ShapeDtypeStruct((batch, n_heads, seq, AUG), BF16),
        jax.ShapeDtypeStruct((batch, n_heads, nb, V_ROWS, tm), BF16),
        jax.ShapeDtypeStruct((n, width), BF16),
        jax.ShapeDtypeStruct((batch, 2, width), F32),
    )
    out_specs = (
        pl.BlockSpec((1, width, tm), lambda b, i: (b, 0, i)),
        pl.BlockSpec((1, width, tm), lambda b, i: (b, 0, i)),
        pl.BlockSpec((1, n_heads, tm), lambda b, i: (b, 0, i)),
        pl.BlockSpec((1, n_heads, AUG, tm), lambda b, i: (b, 0, 0, i)),
        pl.BlockSpec((1, n_heads, tm, AUG), lambda b, i: (b, 0, i, 0)),
        pl.BlockSpec((1, n_heads, 1, V_ROWS, tm), lambda b, i: (b, 0, i, 0, 0)),
        pl.BlockSpec((tm, width), row),
        pl.BlockSpec((1, 2, width), lambda b, i: (b, 0, 0)),
    )
    return pl.pallas_call(
        functools.partial(_proj_prompt_body, n_heads=n_heads, head_dim=head_dim, scale=1.0 / math.sqrt(head_dim)),
        out_shape=out_shape,
        grid=(batch, nb),
        in_specs=[pl.BlockSpec((tm, d), row)] + [_const_spec(a) for a in consts],
        out_specs=out_specs,
        scratch_shapes=[pltpu.VMEM((1, LANES), F32), pltpu.VMEM((8, width), F32)],
        compiler_params=pltpu.CompilerParams(dimension_semantics=("arbitrary", "arbitrary"),
                                             vmem_limit_bytes=VMEM_LIMIT),
        name="proj_prompt",
    )(x, *consts)


def _proj_sample(x, g, w, bf, cw, gco, s1, s2, *, seq, n_heads, head_dim):
    n, d = x.shape
    width = n_heads * head_dim
    r = jnp.arange(n)
    tri = ((r[None, :] <= r[:, None]) & (r[None, :] // seq == r[:, None] // seq)).astype(BF16)
    args = [x, g, w, bf, tri, cw, gco, s1, s2]
    shapes = [(n, width)] * 3 + [(n, n_heads), (n, LANES), (n, width)]
    out_shape = tuple(jax.ShapeDtypeStruct(s, F32) for s in shapes) + (jax.ShapeDtypeStruct((n, width), BF16),)
    return pl.pallas_call(
        functools.partial(_proj_sample_body, n_heads=n_heads, head_dim=head_dim,
                          scale=1.0 / math.sqrt(head_dim), seq=seq),
        out_shape=out_shape,
        compiler_params=pltpu.CompilerParams(vmem_limit_bytes=VMEM_LIMIT),
        name="proj_sample",
    )(*args)


def _attn_prompt_body(qa_ref, ka_ref, va_ref, o_ref, acc_ref, s_ref, ot_ref, *, head_dim):
    i = pl.program_id(2)
    n_h = qa_ref.shape[1]
    tq = qa_ref.shape[3]
    tk = tq
    sub = lax.broadcasted_iota(jnp.int32, (tk, tq), 0)
    lan = lax.broadcasted_iota(jnp.int32, (tk, tq), 1)
    causal = sub <= lan

    def scores(j, hh):
        k_j = ka_ref[0, hh, pl.ds(pl.multiple_of(j * tk, tk), tk), :]
        return _dot(k_j, qa_ref[0, hh])

    def normalised(acc):
        return acc[0:head_dim] / acc[head_dim:head_dim + 1]

    top = []
    s_next = scores(i, 0)
    for hh in range(n_h):
        s = jnp.where(causal, s_next, NEG)
        s_next = scores(i, hh + 1) if hh + 1 < n_h else scores(0, 0)
        top.append(jnp.max(s, axis=0, keepdims=True))
        acc_ref[hh] = _dot(va_ref[0, hh, i], jnp.exp2(s).astype(BF16))
    s_ref[...] = s_next

    def one_pass_block(j, top):
        s_next = s_ref[...]
        new_top = []
        for hh in range(n_h):
            s = s_next
            s_next = scores(j, hh + 1) if hh + 1 < n_h else scores(j + 1, 0)
            new_top.append(jnp.maximum(top[hh], jnp.max(s, axis=0, keepdims=True)))
            acc_ref[hh] += _dot(va_ref[0, hh, j], jnp.exp2(s).astype(BF16))
        s_ref[...] = s_next
        return tuple(new_top)

    top = lax.fori_loop(0, i, one_pass_block, tuple(top))
    o_ref[0] = jnp.concatenate([normalised(acc_ref[hh]) for hh in range(n_h)], axis=0).T
    hi, lo = top[0], top[0]
    for t in top[1:]:
        hi, lo = jnp.maximum(hi, t), jnp.minimum(lo, t)

    @pl.when((jnp.max(hi) > SHIFT_SLACK) | (jnp.min(lo) < -SHIFT_SLACK))
    def _():
        def online_head(hh, _):
            q_t = qa_ref[0, hh]

            def sc(j):
                return _dot(ka_ref[0, hh, pl.ds(pl.multiple_of(j * tk, tk), tk), :], q_t)

            s = jnp.where(causal, sc(i), NEG)
            m = jnp.max(s, axis=0, keepdims=True)
            acc = _dot(va_ref[0, hh, i], jnp.exp2(s - m).astype(BF16))

            def body(j, carry):
                m, acc = carry
                s = sc(j)
                m_new = jnp.maximum(m, jnp.max(s, axis=0, keepdims=True))
                p = jnp.exp2(s - m_new).astype(BF16)
                return m_new, jnp.exp2(m - m_new) * acc + _dot(va_ref[0, hh, j], p)

            m, acc = lax.fori_loop(0, i, body, (m, acc))
            ot_ref[pl.ds(pl.multiple_of(hh * head_dim, head_dim), head_dim), :] = normalised(acc)
            return 0

        lax.fori_loop(0, n_h, online_head, 0)
        o_ref[0] = ot_ref[...].T


def _attn_prompt(qa, ka, va, *, head_dim):
    batch, n_heads, _, seq = qa.shape
    nb, tq = va.shape[2], va.shape[4]
    hps = HEADS_PER_STEP
    resident = dict(pipeline_mode=pl.Buffered(1))
    return pl.pallas_call(
        functools.partial(_attn_prompt_body, head_dim=head_dim),
        out_shape=jax.ShapeDtypeStruct((batch, seq, n_heads * head_dim), F32),
        grid=(batch, n_heads // hps, nb),
        in_specs=[
            pl.BlockSpec((1, hps, AUG, tq), lambda b, g, i: (b, g, 0, i)),
            pl.BlockSpec((1, hps, seq, AUG), lambda b, g, i: (b, g, 0, 0), **resident),
            pl.BlockSpec((1, hps, nb, V_ROWS, tq), lambda b, g, i: (b, g, 0, 0, 0), **resident),
        ],
        out_specs=pl.BlockSpec((1, tq, hps * head_dim), lambda b, g, i: (b, i, g)),
        scratch_shapes=[pltpu.VMEM((hps, V_ROWS, tq), F32), pltpu.VMEM((tq, tq), F32),
                        pltpu.VMEM((hps * head_dim, tq), F32)],
        compiler_params=pltpu.CompilerParams(dimension_semantics=("arbitrary",) * 3, vmem_limit_bytes=VMEM_LIMIT),
        name="attn_prompt",
    )(qa, ka, va)


def _suffix_body(x_ref, rhs_ref, o_ref):
    hi, mid, lo = _split3(x_ref[...])
    rhs = rhs_ref[...]
    o_ref[...] = _dot(hi.astype(BF16), rhs) + _dot(mid.astype(BF16), rhs) + _dot(lo.astype(BF16), rhs)


def _suffix_pool(lf_pool):
    n_pool, n_heads, page = lf_pool.shape
    rows = n_pool * n_heads
    tr = SUFFIX_ROWS
    assert rows % tr == 0
    r = jnp.arange(page)
    rhs = jnp.concatenate([(r[:, None] > r[None, :]).astype(BF16), jnp.ones((page, page), BF16)], axis=1)
    out = pl.pallas_call(
        _suffix_body,
        out_shape=jax.ShapeDtypeStruct((rows, 2 * page), F32),
        grid=(rows // tr,),
        in_specs=[pl.BlockSpec((tr, page), lambda i: (i, 0)), _const_spec(rhs)],
        out_specs=pl.BlockSpec((tr, 2 * page), lambda i: (i, 0)),
        compiler_params=pltpu.CompilerParams(dimension_semantics=("arbitrary",), vmem_limit_bytes=VMEM_LIMIT),
        name="suffix_pool",
    )(lf_pool.reshape(rows, page), rhs)
    return out.reshape(n_pool, n_heads, 2 * page)


_N_SAMPLE_IN = 9


def _sample_chunk_pair(t, pt_ref, qbd_ref, cq_ref, knew_ref, vnew_ref, cnew_ref, kt_hbm, vt_hbm, sfx_hbm, o_ref,
                       kbuf, vbuf, sbuf, sem, m_ref, l_ref, lsum_ref, top_ref, acc_ref, carry_ref,
                       *, g, n_pages, seq, n_heads, head_dim):
    dec_batch = qbd_ref.shape[0]
    n_new = knew_ref.shape[1]
    per_row = n_pages // g
    n_chunks = dec_batch * per_row

    def page_copies(c, slot, p_i):
        page = pt_ref[(c // per_row) * n_pages + (n_pages - 1) - (c % per_row) * g - p_i]
        return (pltpu.make_async_copy(kt_hbm.at[page], kbuf.at[slot, p_i], sem.at[0, slot]),
                pltpu.make_async_copy(vt_hbm.at[page], vbuf.at[slot, p_i], sem.at[1, slot]),
                pltpu.make_async_copy(sfx_hbm.at[page], sbuf.at[slot, p_i], sem.at[2, slot]))

    def start_chunk(c, slot):
        def body(p_i, _):
            for cp in page_copies(c, slot, p_i):
                cp.start()
            return 0
        lax.fori_loop(0, g, body, 0)

    def wait_chunk(c, slot):
        def body(p_i, _):
            for cp in page_copies(c, slot, p_i):
                cp.wait()
            return 0
        lax.fori_loop(0, g, body, 0)

    def start_row(b):
        qb, cq = qbd_ref[b], cq_ref[b]
        s = _dot_nt(qb, knew_ref[b]) + cq[:, 0:n_new] - cnew_ref[b]
        t_row = lax.broadcasted_iota(jnp.int32, s.shape, 0) // n_heads
        j = lax.broadcasted_iota(jnp.int32, s.shape, 1)
        s = jnp.where((j <= t_row) & (j < seq), s, NEG)
        m = jnp.max(s, axis=1, keepdims=True)
        p = jnp.exp(s - m)
        m_ref[...] = m
        l_ref[...] = jnp.sum(p, axis=1, keepdims=True)
        acc_ref[...] = _dot(p.astype(BF16), vnew_ref[b])
        lsum_ref[...] = jnp.zeros_like(lsum_ref)
        top_ref[...] = jnp.full_like(top_ref, NEG)
        carry_ref[...] = jnp.zeros_like(carry_ref)

    def page_scores(b, slot):
        qb, cq = qbd_ref[b], cq_ref[b]
        carry = carry_ref[...]
        parts = []
        for p_i in range(g):
            sfx = sbuf[slot, p_i]
            suffix = sfx[:, 0:LANES] + carry
            carry = carry + sfx[:, LANES:2 * LANES]
            bias = jnp.concatenate([suffix] * seq, axis=0) + cq
            parts.append(_dot(qb, kbuf[slot, p_i].astype(BF16)) + bias)
        carry_ref[...] = carry
        return parts

    def one_pass_chunk(b, slot):
        m = m_ref[...]
        top, lsum, acc = top_ref[...], lsum_ref[...], acc_ref[...]
        for p_i, s in enumerate(page_scores(b, slot)):
            top = jnp.maximum(top, s)
            p = jnp.exp(s - m)
            lsum = lsum + p
            acc = acc + _dot_nt(p.astype(BF16), vbuf[slot, p_i].astype(BF16))
        top_ref[...], lsum_ref[...], acc_ref[...] = top, lsum, acc

    def online_chunk(b, slot):
        s = jnp.concatenate(page_scores(b, slot), axis=1)
        m_prev = m_ref[...]
        m_new = jnp.maximum(m_prev, jnp.max(s, axis=1, keepdims=True))
        p = jnp.exp(s - m_new)
        alpha = jnp.exp(m_prev - m_new)
        l_ref[...] = alpha * l_ref[...] + jnp.sum(p, axis=1, keepdims=True)
        pb = p.astype(BF16)
        acc = alpha * acc_ref[...]
        for p_i in range(g):
            acc = acc + _dot_nt(pb[:, p_i * LANES:(p_i + 1) * LANES], vbuf[slot, p_i].astype(BF16))
        acc_ref[...] = acc
        m_ref[...] = m_new

    def write_row(b, l):
        o = acc_ref[...] / l
        r = lax.broadcasted_iota(jnp.int32, o.shape, 0)
        col = lax.broadcasted_iota(jnp.int32, o.shape, 1)
        o = jnp.where(col // head_dim == r % n_heads, o, 0.0)
        o_ref[b] = jnp.sum(o.reshape(seq, n_heads, o.shape[1]), axis=1)

    def finish_row(b, slot):
        write_row(b, l_ref[...] + jnp.sum(lsum_ref[...], axis=1, keepdims=True))
        excess = jnp.max(top_ref[...], axis=1, keepdims=True) - m_ref[...]

        @pl.when(jnp.max(excess) > SAMPLE_SLACK)
        def _():
            start_row(b)

            def redo(r, _):
                start_chunk(b * per_row + r, slot)
                wait_chunk(b * per_row + r, slot)
                online_chunk(b, slot)
                return 0

            lax.fori_loop(0, per_row, redo, 0)
            write_row(b, l_ref[...])

    @pl.when(t == 0)
    def _():
        start_chunk(0, 0)
        start_chunk(1, 1)

    for slot in (0, 1):
        c = 2 * t + slot
        b, r = c // per_row, c % per_row

        @pl.when(r == 0)
        def _():
            start_row(b)

        wait_chunk(c, slot)
        one_pass_chunk(b, slot)

        if slot == 1:
            @pl.when(r == per_row - 1)
            def _():
                finish_row(b, slot)

        @pl.when(c + 2 < n_chunks)
        def _():
            start_chunk(c + 2, slot)


def _sample_specs(pt_flat, qbd, cq, knew, vnew, cnew, kt_pool, vt_pool, sfx_pool, *, g, n_pages, seq, n_heads,
                  head_dim):
    dec_batch, rows, width = qbd.shape
    page = kt_pool.shape[2]
    assert n_pages % (2 * g) == 0 and pt_flat.shape[0] == dec_batch * n_pages
    vmem = pl.BlockSpec(memory_space=pltpu.VMEM)
    hbm = pl.BlockSpec(memory_space=pl.ANY)
    in_specs = [pl.BlockSpec(memory_space=pltpu.SMEM)] + [vmem] * 5 + [hbm] * 3
    scratch = [
        pltpu.VMEM((2, g, width, page), F32), pltpu.VMEM((2, g, width, page), F32),
        pltpu.VMEM((2, g, n_heads, 2 * page), F32), pltpu.SemaphoreType.DMA((3, 2)),
        pltpu.VMEM((rows, 1), F32), pltpu.VMEM((rows, 1), F32), pltpu.VMEM((rows, LANES), F32),
        pltpu.VMEM((rows, LANES), F32), pltpu.VMEM((rows, width), F32), pltpu.VMEM((n_heads, LANES), F32),
    ]
    return in_specs, jax.ShapeDtypeStruct((dec_batch, seq, width), F32), scratch, dec_batch * n_pages // (2 * g)


def kernel(x_prompt, x_sample, cache_k, cache_v, cache_logf, state_conv, page_table, g_ffn1, w1_ffn1, w3_ffn1,
           w2_ffn1, g_mix, w_in, b_f, conv_w, g_attn_out, g_conv_out, w_out, g_ffn2, w1_ffn2, w3_ffn2, w2_ffn2,
           g_final):
    batch, seq, d_model = x_prompt.shape
    dec_batch, dec_seq, _ = x_sample.shape
    depth, n_pool, page, n_heads, head_dim = cache_k.shape
    width = n_heads * head_dim
    assert depth == 1 and seq % ROW_TILE == 0 and page == LANES and 2 * head_dim == LANES
    assert w_in.shape[2] == 6 * width + n_heads

    xp = x_prompt.reshape(batch * seq, d_model)
    xs = x_sample.reshape(dec_batch * dec_seq, d_model)
    row = lambda a: a.reshape(1, -1)
    l = 0

    bw = lambda a: a.astype(BF16)
    wi = w_in[l]
    wf = jnp.tile(wi[:, 3 * width:3 * width + n_heads], (1, 3))
    w_proj = bw(jnp.concatenate([wi[:, :3 * width], wi[:, 3 * width + n_heads:],
                                 jnp.pad(wf, ((0, 0), (0, LANES - 3 * n_heads)))], axis=1))
    bf = jnp.pad(jnp.tile(b_f[l], 3), (0, LANES - 3 * n_heads)).reshape(1, LANES)
    ffn1 = (row(g_ffn1[l]), bw(w1_ffn1[l]), bw(w3_ffn1[l]), bw(w2_ffn1[l]))
    ffn2 = (row(g_ffn2[l]), bw(w1_ffn2[l]), bw(w3_ffn2[l]), bw(w2_ffn2[l]))
    wo = bw(w_out[l])
    merge_w = (row(g_attn_out[l]), wo[:width], wo[width:])
    gmix, cw, gco, gfin = row(g_mix[l]), conv_w[l], row(g_conv_out[l]), row(g_final)

    xs1 = _ffn(xs, *ffn1)
    st = state_conv[l]
    s1 = jnp.repeat(st[:, 1:2], dec_seq, axis=1).reshape(dec_batch * dec_seq, width)
    s2 = jnp.pad(st, ((0, 0), (0, dec_seq - 2), (0, 0))).reshape(dec_batch * dec_seq, width)
    q_s, k_s, v_s, logf_s, c_s, uc_s, cn_s = _proj_sample(
        xs1, gmix, w_proj, bf, cw, gco, s1, s2, seq=dec_seq, n_heads=n_heads, head_dim=head_dim)

    eye = jnp.eye(n_heads, dtype=F32)
    q4 = q_s.reshape(dec_batch, dec_seq, n_heads, 1, head_dim)
    qbd = bw((q4 * eye[None, None, :, :, None]).reshape(dec_batch, dec_seq * n_heads, width))
    c4 = c_s[:, :n_heads].reshape(dec_batch, dec_seq, n_heads)
    cq = jnp.broadcast_to(c4.reshape(dec_batch, dec_seq * n_heads, 1), (dec_batch, dec_seq * n_heads, LANES))
    n_new = 16
    cnew = jnp.broadcast_to(jnp.transpose(c4, (0, 2, 1))[:, None], (dec_batch, dec_seq, n_heads, dec_seq))
    cnew = jnp.pad(cnew.reshape(dec_batch, dec_seq * n_heads, dec_seq), ((0, 0), (0, 0), (0, n_new - dec_seq)))
    pad_new = lambda a: bw(jnp.pad(a.reshape(dec_batch, dec_seq, width), ((0, 0), (0, n_new - dec_seq), (0, 0))))
    kt_pool = jnp.transpose(cache_k[l], (0, 2, 3, 1)).reshape(n_pool, width, page)
    vt_pool = jnp.transpose(cache_v[l], (0, 2, 3, 1)).reshape(n_pool, width, page)
    sfx_pool = _suffix_pool(jnp.transpose(cache_logf[l], (0, 2, 1)))
    n_pages = page_table.shape[1]
    half = dec_batch // 2
    static = dict(g=PAGES_PER_STEP, n_pages=n_pages, seq=dec_seq, n_heads=n_heads, head_dim=head_dim)

    def sample_part(lo):
        rows = slice(lo, lo + half)
        per_row = (page_table[rows].reshape(-1), qbd[rows], cq[rows], pad_new(k_s)[rows], pad_new(v_s)[rows],
                   cnew[rows])
        return per_row + (kt_pool, vt_pool, sfx_pool), static

    xp1, o_s_lo = _ffn(xp, *ffn1, sample=sample_part(0))
    kt_p, vt_p, logft_p, qa, ka, va, cn_p, tail_p = _proj_prompt(
        xp1, gmix, w_proj, bf, cw, gco, batch=batch, n_heads=n_heads, head_dim=head_dim)
    o_p = _attn_prompt(qa, ka, va, head_dim=head_dim).reshape(batch * seq, width)
    y_p, o_s_hi = _ffn(xp1, *ffn2, merge=(o_p, cn_p) + merge_w, g_final=gfin, sample=sample_part(half))

    o_s = jnp.concatenate([o_s_lo, o_s_hi], axis=0).reshape(dec_batch * dec_seq, width)
    y_s = _ffn(xs1, *ffn2, merge=(o_s, cn_s) + merge_w, g_final=gfin)

    hd = (n_heads, head_dim)
    untranspose = lambda a: jnp.transpose(a.reshape(batch, n_heads, head_dim, seq), (0, 3, 1, 2))[None]
    return (
        y_p.reshape(batch, seq, d_model),
        y_s.reshape(dec_batch, dec_seq, d_model),
        untranspose(kt_p),
        untranspose(vt_p),
        jnp.transpose(logft_p, (0, 2, 1))[None],
        tail_p.reshape(1, batch, 2, width),
        k_s.reshape(1, dec_batch, dec_seq, *hd),
        v_s.reshape(1, dec_batch, dec_seq, *hd),
        logf_s.reshape(1, dec_batch, dec_seq, n_heads),
        uc_s.reshape(1, dec_batch, dec_seq, width)[:, :, dec_seq - 2:],
    )
```

```python
import functools
import math

import jax
import jax.numpy as jnp
from jax import lax
from jax.experimental import pallas as pl
from jax.experimental.pallas import tpu as pltpu

F32 = jnp.float32
BF16 = jnp.bfloat16
EPS = 1e-6
NEG = -1e30
LANES = 128
ROW_TILE = 512
SAMPLE_ROW_TILE = 256
PAGES_PER_STEP = 16
PAGE_SLOTS = 3
HEADS_PER_STEP = 8
SHIFT_SLACK = 64.0
SAMPLE_SLACK = 40.0
SUFFIX_ROWS = 2048
LOG2E = 1.4426950408889634
VMEM_LIMIT = 60 * 1024 * 1024
AUG = 128
V_ROWS = 80


def _dot(a, b):
    return jnp.dot(a, b, preferred_element_type=F32)


def _dot_nt(a, b):
    return lax.dot_general(a, b, (((1,), (1,)), ((), ())), preferred_element_type=F32)


def _rms(x, g):
    return x * lax.rsqrt(jnp.mean(x * x, axis=-1, keepdims=True) + EPS) * g


def _split3(x):
    hi = x.astype(BF16).astype(F32)
    r1 = x - hi
    mid = r1.astype(BF16).astype(F32)
    lo = (r1 - mid).astype(BF16).astype(F32)
    return hi, mid, lo


def _const_spec(a):
    nd = a.ndim
    return pl.BlockSpec(a.shape, lambda *_: (0,) * nd, pipeline_mode=pl.Buffered(1))


def _ffn_body(*refs, merge, final, sample):
    x_ref, g_ref, w1_ref, w3_ref, w2_ref = refs[:5]
    rest = refs[5:]
    if merge:
        o_ref, cn_ref, gao_ref, woa_ref, woc_ref = rest[:5]
        rest = rest[5:]
    if final:
        gfin_ref = rest[0]
        rest = rest[1:]
    if sample:
        sample_in, rest = rest[:_N_SAMPLE_IN], rest[_N_SAMPLE_IN:]
        out_ref, os_ref = rest[:2]
        _sample_chunk_pair(pl.program_id(0), *sample_in, os_ref, *rest[2:], **sample)
    else:
        out_ref = rest[0]
    x = x_ref[...]
    if merge:
        an = _rms(o_ref[...], gao_ref[...]).astype(BF16)
        x = x + (_dot(an, woa_ref[...]) + _dot(cn_ref[...], woc_ref[...]))
    h = _rms(x, g_ref[...]).astype(BF16)
    a = _dot(h, w1_ref[...])
    b = _dot(h, w3_ref[...])
    act = (a * jax.nn.sigmoid(a) * b).astype(BF16)
    y = x + 0.5 * _dot(act, w2_ref[...])
    if final:
        y = _rms(y, gfin_ref[...])
    out_ref[...] = y


def _ffn(x, g, w1, w3, w2, *, merge=None, g_final=None, sample=None):
    n, d = x.shape
    tm = min(ROW_TILE if sample is None else SAMPLE_ROW_TILE, n)
    row = lambda i: (i, 0)
    args = [x, g, w1, w3, w2]
    specs = [pl.BlockSpec((tm, d), row)] + [_const_spec(a) for a in args[1:]]
    if merge is not None:
        o, cn, gao, woa, woc = merge
        args += [o, cn, gao, woa, woc]
        specs += [pl.BlockSpec((tm, o.shape[1]), row), pl.BlockSpec((tm, cn.shape[1]), row),
                  _const_spec(gao), _const_spec(woa), _const_spec(woc)]
    if g_final is not None:
        args.append(g_final)
        specs.append(_const_spec(g_final))
    out_shape = jax.ShapeDtypeStruct((n, d), F32)
    out_specs = pl.BlockSpec((tm, d), row)
    scratch, static = [], None
    if sample is not None:
        operands, static = sample
        s_specs, s_out_shape, scratch, n_steps = _sample_specs(*operands, **static)
        assert n // tm == n_steps and len(operands) == _N_SAMPLE_IN
        args += list(operands)
        specs += s_specs
        out_shape = (out_shape, s_out_shape)
        out_specs = (out_specs, pl.BlockSpec(memory_space=pltpu.VMEM))
    return pl.pallas_call(
        functools.partial(_ffn_body, merge=merge is not None, final=g_final is not None, sample=static),
        out_shape=out_shape,
        grid=(n // tm,),
        in_specs=specs,
        out_specs=out_specs,
        scratch_shapes=scratch,
        compiler_params=pltpu.CompilerParams(dimension_semantics=("arbitrary",), vmem_limit_bytes=VMEM_LIMIT),
        name=("ffn_merge" if merge is not None else "ffn") + ("_sample" if sample is not None else ""),
    )(*args)


def _proj_common(x_ref, g_ref, w_ref, bf_ref, tri_ref, width, scale):
    a = width
    h = _rms(x_ref[...], g_ref[...]).astype(BF16)
    p = _dot(h, w_ref[...])
    q = p[:, 0:a] * scale
    k = p[:, a:2 * a]
    v = p[:, 2 * a:3 * a]
    bg = p[:, 3 * a:4 * a]
    uc = p[:, 4 * a:5 * a] * p[:, 5 * a:6 * a]
    z = p[:, 6 * a:6 * a + LANES] + bf_ref[...]
    lane = lax.broadcasted_iota(jnp.int32, z.shape, 1)
    logf = jnp.minimum(z, 0.0) - jnp.log1p(jnp.exp(-jnp.abs(z)))
    l3 = jnp.where(lane < 24, logf, 0.0)
    hi, mid, lo = _split3(l3)
    packed = jnp.where(lane < 8, hi, jnp.where(lane < 16, mid, lo)).astype(BF16)
    r = _dot(tri_ref[...], packed)
    c = r + pltpu.roll(r, LANES - 8, axis=1) + pltpu.roll(r, LANES - 16, axis=1)
    c = jnp.where(lane < 8, c, 0.0)
    return q, k, v, bg, uc, logf, c, lane


def _conv_branch(uc, prev1, prev2, bg, cw_ref, gco_ref):
    y = cw_ref[0:1, :] * prev2 + cw_ref[1:2, :] * prev1 + cw_ref[2:3, :] * uc
    return _rms(bg * y, gco_ref[...]).astype(BF16)


def _proj_prompt_body(x_ref, g_ref, w_ref, bf_ref, tri_ref, cw_ref, gco_ref, psel_ref, asel_ref, hsel_ref,
                      kt_ref, vt_ref, logf_ref, qa_ref, ka_ref, va_ref, cn_ref, tail_ref,
                      carry_c, carry_u, *, n_heads, head_dim, scale):
    tm = x_ref.shape[0]
    width = n_heads * head_dim

    @pl.when(pl.program_id(1) == 0)
    def _():
        carry_c[...] = jnp.zeros_like(carry_c)
        carry_u[...] = jnp.zeros_like(carry_u)

    q, k, v, bg, uc, logf, c, lane = _proj_common(x_ref, g_ref, w_ref, bf_ref, tri_ref, width, scale * LOG2E)
    q_t = q.T
    v_t = v.T
    kt_ref[0] = k.T
    vt_ref[0] = v_t
    logf_ref[0] = logf.T[0:n_heads]

    c = c + carry_c[...]
    carry_c[...] = c[tm - 1:tm, :]
    c2 = c * LOG2E

    def packed_parts(x):
        rep = jnp.where(lane < 8, x, jnp.where(lane < 16, pltpu.roll(x, 8, axis=1),
                                               jnp.where(lane < 24, pltpu.roll(x, 16, axis=1), 0.0)))
        hi, mid, lo = _split3(rep)
        return jnp.where(lane < 8, hi, jnp.where(lane < 16, mid, jnp.where(lane < 24, lo,
                         jnp.where(lane == 24, 1.0, 0.0)))).astype(BF16)

    shift = _dot((q * k).astype(BF16), hsel_ref[...])
    k_extra = _dot(packed_parts(c2), psel_ref[...])
    q_extra = _dot_nt(asel_ref[...], packed_parts(c2 - shift))

    lane_a = lax.broadcasted_iota(jnp.int32, (tm, AUG), 1)
    row8 = lax.broadcasted_iota(jnp.int32, (8, tm), 0)
    ones_row = jnp.where(row8 == 0, 1.0, 0.0)
    pad_q = jnp.zeros((AUG - head_dim - 8, tm), F32)
    pad_v = jnp.zeros((V_ROWS - head_dim - 8, tm), F32)
    for h in range(n_heads):
        blk = k[:, (h // 2) * AUG:(h // 2 + 1) * AUG]
        if h % 2:
            blk = pltpu.roll(blk, head_dim, axis=1)
        ka_ref[0, h] = jnp.where(lane_a < head_dim, blk, k_extra[:, h * AUG:(h + 1) * AUG]).astype(BF16)
        qa_ref[0, h] = jnp.concatenate(
            [q_t[h * head_dim:(h + 1) * head_dim], q_extra[h * 8:(h + 1) * 8], pad_q], axis=0).astype(BF16)
        va_ref[0, h, 0] = jnp.concatenate(
            [v_t[h * head_dim:(h + 1) * head_dim], ones_row, pad_v], axis=0).astype(BF16)

    rows = lax.broadcasted_iota(jnp.int32, uc.shape, 0)
    last1 = carry_u[7:8, :]
    last2 = carry_u[6:7, :]
    prev1 = jnp.where(rows == 0, last1, pltpu.roll(uc, 1, axis=0))
    prev2 = jnp.where(rows == 0, last2, jnp.where(rows == 1, last1, pltpu.roll(uc, 2, axis=0)))
    cn_ref[...] = _conv_branch(uc, prev1, prev2, bg, cw_ref, gco_ref)
    tail_ref[0] = uc[tm - 2:tm, :]
    carry_u[...] = uc[tm - 8:tm, :]


def _proj_sample_body(x_ref, g_ref, w_ref, bf_ref, tri_ref, cw_ref, gco_ref, s1_ref, s2_ref,
                      q_ref, k_ref, v_ref, logf_ref, c_ref, uc_ref, cn_ref, *, n_heads, head_dim, scale, seq):
    q, k, v, bg, uc, logf, c, _ = _proj_common(x_ref, g_ref, w_ref, bf_ref, tri_ref, n_heads * head_dim, scale)
    q_ref[...] = q
    k_ref[...] = k
    v_ref[...] = v
    logf_ref[...] = logf[:, 0:n_heads]
    c_ref[...] = c
    uc_ref[...] = uc
    t = lax.broadcasted_iota(jnp.int32, uc.shape, 0) % seq
    prev1 = jnp.where(t == 0, s1_ref[...], pltpu.roll(uc, 1, axis=0))
    prev2 = jnp.where(t < 2, s2_ref[...], pltpu.roll(uc, 2, axis=0))
    cn_ref[...] = _conv_branch(uc, prev1, prev2, bg, cw_ref, gco_ref)


def _selectors(n_heads):
    psel = jnp.zeros((LANES, n_heads * AUG), F32)
    asel = jnp.zeros((n_heads * 8, LANES), F32)
    for h in range(n_heads):
        base = h * AUG + 64
        psel = psel.at[24, base:base + 3].set(1.0)
        for part in range(3):
            psel = psel.at[part * 8 + h, base + 3 + part].set(-1.0)
            asel = asel.at[h * 8 + part, part * 8 + h].set(1.0)
        asel = asel.at[h * 8 + 3:h * 8 + 6, 24].set(1.0)
    return psel.astype(BF16), asel.astype(BF16)


def _proj_prompt(x, g, w, bf, cw, gco, *, batch, n_heads, head_dim):
    n, d = x.shape
    seq = n // batch
    tm = ROW_TILE
    nb = seq // tm
    width = n_heads * head_dim
    tri = jnp.tril(jnp.ones((tm, tm), F32)).astype(BF16)
    psel, asel = _selectors(n_heads)
    hsel = (jnp.arange(width)[:, None] // head_dim == jnp.arange(LANES)[None, :]).astype(BF16)
    row = lambda b, i: (b * nb + i, 0)
    consts = [g, w, bf, tri, cw, gco, psel, asel, hsel]
    out_shape = (
        jax.ShapeDtypeStruct((batch, width, seq), F32),
        jax.ShapeDtypeStruct((batch, width, seq), F32),
        jax.ShapeDtypeStruct((batch, n_heads, seq), F32),
        jax.ShapeDtypeStruct((batch, n_heads, AUG, seq), BF16),
        jax.ShapeDtypeStruct((batch, n_heads, seq, AUG), BF16),
        jax.ShapeDtypeStruct((batch, n_heads, nb, V_ROWS, tm), BF16),
        jax.ShapeDtypeStruct((n, width), BF16),
        jax.ShapeDtypeStruct((batch, 2, width), F32),
    )
    out_specs = (
        pl.BlockSpec((1, width, tm), lambda b, i: (b, 0, i)),
        pl.BlockSpec((1, width, tm), lambda b, i: (b, 0, i)),
        pl.BlockSpec((1, n_heads, tm), lambda b, i: (b, 0, i)),
        pl.BlockSpec((1, n_heads, AUG, tm), lambda b, i: (b, 0, 0, i)),
        pl.BlockSpec((1, n_heads, tm, AUG), lambda b, i: (b, 0, i, 0)),
        pl.BlockSpec((1, n_heads, 1, V_ROWS, tm), lambda b, i: (b, 0, i, 0, 0)),
        pl.BlockSpec((tm, width), row),
        pl.BlockSpec((1, 2, width), lambda b, i: (b, 0, 0)),
    )
    return pl.pallas_call(
        functools.partial(_proj_prompt_body, n_heads=n_heads, head_dim=head_dim, scale=1.0 / math.sqrt(head_dim)),
        out_shape=out_shape,
        grid=(batch, nb),
        in_specs=[pl.BlockSpec((tm, d), row)] + [_const_spec(a) for a in consts],
        out_specs=out_specs,
        scratch_shapes=[pltpu.VMEM((1, LANES), F32), pltpu.VMEM((8, width), F32)],
        compiler_params=pltpu.CompilerParams(dimension_semantics=("arbitrary", "arbitrary"),
                                             vmem_limit_bytes=VMEM_LIMIT),
        name="proj_prompt",
    )(x, *consts)


def _proj_sample(x, g, w, bf, cw, gco, s1, s2, *, seq, n_heads, head_dim):
    n, d = x.shape
    width = n_heads * head_dim
    r = jnp.arange(n)
    tri = ((r[None, :] <= r[:, None]) & (r[None, :] // seq == r[:, None] // seq)).astype(BF16)
    args = [x, g, w, bf, tri, cw, gco, s1, s2]
    shapes = [(n, width)] * 3 + [(n, n_heads), (n, LANES), (n, width)]
    out_shape = tuple(jax.ShapeDtypeStruct(s, F32) for s in shapes) + (jax.ShapeDtypeStruct((n, width), BF16),)
    return pl.pallas_call(
        functools.partial(_proj_sample_body, n_heads=n_heads, head_dim=head_dim,
                          scale=1.0 / math.sqrt(head_dim), seq=seq),
        out_shape=out_shape,
        compiler_params=pltpu.CompilerParams(vmem_limit_bytes=VMEM_LIMIT),
        name="proj_sample",
    )(*args)


def _attn_prompt_body(qa_ref, ka_ref, va_ref, o_ref, acc_ref, s_ref, ot_ref, *, head_dim):
    i = pl.program_id(2)
    n_h = qa_ref.shape[1]
    tq = qa_ref.shape[3]
    tk = tq
    sub = lax.broadcasted_iota(jnp.int32, (tk, tq), 0)
    lan = lax.broadcasted_iota(jnp.int32, (tk, tq), 1)
    causal = sub <= lan

    def scores(j, hh):
        k_j = ka_ref[0, hh, pl.ds(pl.multiple_of(j * tk, tk), tk), :]
        return _dot(k_j, qa_ref[0, hh])

    def normalised(acc):
        return acc[0:head_dim] / acc[head_dim:head_dim + 1]

    top = []
    s_next = scores(i, 0)
    for hh in range(n_h):
        s = jnp.where(causal, s_next, NEG)
        s_next = scores(i, hh + 1) if hh + 1 < n_h else scores(0, 0)
        top.append(jnp.max(s, axis=0, keepdims=True))
        acc_ref[hh] = _dot(va_ref[0, hh, i], jnp.exp2(s).astype(BF16))
    s_ref[...] = s_next

    def one_pass_block(j, top):
        s_next = s_ref[...]
        new_top = []
        for hh in range(n_h):
            s = s_next
            s_next = scores(j, hh + 1) if hh + 1 < n_h else scores(j + 1, 0)
            new_top.append(jnp.maximum(top[hh], jnp.max(s, axis=0, keepdims=True)))
            acc_ref[hh] += _dot(va_ref[0, hh, j], jnp.exp2(s).astype(BF16))
        s_ref[...] = s_next
        return tuple(new_top)

    top = lax.fori_loop(0, i, one_pass_block, tuple(top))
    o_ref[0] = jnp.concatenate([normalised(acc_ref[hh]) for hh in range(n_h)], axis=0).T
    hi, lo = top[0], top[0]
    for t in top[1:]:
        hi, lo = jnp.maximum(hi, t), jnp.minimum(lo, t)

    @pl.when((jnp.max(hi) > SHIFT_SLACK) | (jnp.min(lo) < -SHIFT_SLACK))
    def _():
        def online_head(hh, _):
            q_t = qa_ref[0, hh]

            def sc(j):
                return _dot(ka_ref[0, hh, pl.ds(pl.multiple_of(j * tk, tk), tk), :], q_t)

            s = jnp.where(causal, sc(i), NEG)
            m = jnp.max(s, axis=0, keepdims=True)
            acc = _dot(va_ref[0, hh, i], jnp.exp2(s - m).astype(BF16))

            def body(j, carry):
                m, acc = carry
                s = sc(j)
                m_new = jnp.maximum(m, jnp.max(s, axis=0, keepdims=True))
                p = jnp.exp2(s - m_new).astype(BF16)
                return m_new, jnp.exp2(m - m_new) * acc + _dot(va_ref[0, hh, j], p)

            m, acc = lax.fori_loop(0, i, body, (m, acc))
            ot_ref[pl.ds(pl.multiple_of(hh * head_dim, head_dim), head_dim), :] = normalised(acc)
            return 0

        lax.fori_loop(0, n_h, online_head, 0)
        o_ref[0] = ot_ref[...].T


def _attn_prompt(qa, ka, va, *, head_dim):
    batch, n_heads, _, seq = qa.shape
    nb, tq = va.shape[2], va.shape[4]
    hps = HEADS_PER_STEP
    resident = dict(pipeline_mode=pl.Buffered(1))
    return pl.pallas_call(
        functools.partial(_attn_prompt_body, head_dim=head_dim),
        out_shape=jax.ShapeDtypeStruct((batch, seq, n_heads * head_dim), F32),
        grid=(batch, n_heads // hps, nb),
        in_specs=[
            pl.BlockSpec((1, hps, AUG, tq), lambda b, g, i: (b, g, 0, i)),
            pl.BlockSpec((1, hps, seq, AUG), lambda b, g, i: (b, g, 0, 0), **resident),
            pl.BlockSpec((1, hps, nb, V_ROWS, tq), lambda b, g, i: (b, g, 0, 0, 0), **resident),
        ],
        out_specs=pl.BlockSpec((1, tq, hps * head_dim), lambda b, g, i: (b, i, g)),
        scratch_shapes=[pltpu.VMEM((hps, V_ROWS, tq), F32), pltpu.VMEM((tq, tq), F32),
                        pltpu.VMEM((hps * head_dim, tq), F32)],
        compiler_params=pltpu.CompilerParams(dimension_semantics=("arbitrary",) * 3, vmem_limit_bytes=VMEM_LIMIT),
        name="attn_prompt",
    )(qa, ka, va)


def _suffix_body(x_ref, rhs_ref, o_ref):
    hi, mid, lo = _split3(x_ref[...])
    rhs = rhs_ref[...]
    o_ref[...] = _dot(hi.astype(BF16), rhs) + _dot(mid.astype(BF16), rhs) + _dot(lo.astype(BF16), rhs)


def _suffix_pool(lf_pool):
    n_pool, n_heads, page = lf_pool.shape
    rows = n_pool * n_heads
    tr = SUFFIX_ROWS
    assert rows % tr == 0
    r = jnp.arange(page)
    rhs = jnp.concatenate([(r[:, None] > r[None, :]).astype(BF16), jnp.ones((page, page), BF16)], axis=1)
    out = pl.pallas_call(
        _suffix_body,
        out_shape=jax.ShapeDtypeStruct((rows, 2 * page), F32),
        grid=(rows // tr,),
        in_specs=[pl.BlockSpec((tr, page), lambda i: (i, 0)), _const_spec(rhs)],
        out_specs=pl.BlockSpec((tr, 2 * page), lambda i: (i, 0)),
        compiler_params=pltpu.CompilerParams(dimension_semantics=("arbitrary",), vmem_limit_bytes=VMEM_LIMIT),
        name="suffix_pool",
    )(lf_pool.reshape(rows, page), rhs)
    return out.reshape(n_pool, n_heads, 2 * page)


_N_SAMPLE_IN = 9


def _sample_chunk_pair(t, pt_ref, qbd_ref, cq_ref, knew_ref, vnew_ref, cnew_ref, kt_hbm, vt_hbm, sfx_hbm, o_ref,
                       kbuf, vbuf, sbuf, sem, m_ref, l_ref, lsum_ref, top_ref, acc_ref, carry_ref,
                       *, g, n_pages, seq, n_heads, head_dim):
    dec_batch = qbd_ref.shape[0]
    n_new = knew_ref.shape[1]
    per_row = n_pages // g
    n_chunks = dec_batch * per_row

    def page_copies(c, slot, p_i):
        page = pt_ref[(c // per_row) * n_pages + (n_pages - 1) - (c % per_row) * g - p_i]
        return (pltpu.make_async_copy(kt_hbm.at[page], kbuf.at[slot, p_i], sem.at[0, slot]),
                pltpu.make_async_copy(vt_hbm.at[page], vbuf.at[slot, p_i], sem.at[1, slot]),
                pltpu.make_async_copy(sfx_hbm.at[page], sbuf.at[slot, p_i], sem.at[2, slot]))

    def start_chunk(c, slot):
        def body(p_i, _):
            for cp in page_copies(c, slot, p_i):
                cp.start()
            return 0
        lax.fori_loop(0, g, body, 0)

    def wait_chunk(c, slot):
        def body(p_i, _):
            for cp in page_copies(c, slot, p_i):
                cp.wait()
            return 0
        lax.fori_loop(0, g, body, 0)

    def start_row(b):
        qb, cq = qbd_ref[b], cq_ref[b]
        s = _dot_nt(qb, knew_ref[b]) + cq[:, 0:n_new] - cnew_ref[b]
        t_row = lax.broadcasted_iota(jnp.int32, s.shape, 0) // n_heads
        j = lax.broadcasted_iota(jnp.int32, s.shape, 1)
        s = jnp.where((j <= t_row) & (j < seq), s, NEG)
        m = jnp.max(s, axis=1, keepdims=True)
        p = jnp.exp(s - m)
        m_ref[...] = m
        l_ref[...] = jnp.sum(p, axis=1, keepdims=True)
        acc_ref[...] = _dot(p.astype(BF16), vnew_ref[b])
        lsum_ref[...] = jnp.zeros_like(lsum_ref)
        top_ref[...] = jnp.full_like(top_ref, NEG)
        carry_ref[...] = jnp.zeros_like(carry_ref)

    def page_scores(b, slot):
        qb, cq = qbd_ref[b], cq_ref[b]
        carry = carry_ref[...]
        parts = []
        for p_i in range(g):
            sfx = sbuf[slot, p_i]
            suffix = sfx[:, 0:LANES] + carry
            carry = carry + sfx[:, LANES:2 * LANES]
            bias = jnp.concatenate([suffix] * seq, axis=0) + cq
            parts.append(_dot(qb, kbuf[slot, p_i].astype(BF16)) + bias)
        carry_ref[...] = carry
        return parts

    def one_pass_chunk(b, slot):
        m = m_ref[...]
        top, lsum, acc = top_ref[...], lsum_ref[...], acc_ref[...]
        for p_i, s in enumerate(page_scores(b, slot)):
            top = jnp.maximum(top, s)
            p = jnp.exp(s - m)
            lsum = lsum + p
            acc = acc + _dot_nt(p.astype(BF16), vbuf[slot, p_i].astype(BF16))
        top_ref[...], lsum_ref[...], acc_ref[...] = top, lsum, acc

    def online_chunk(b, slot):
        s = jnp.concatenate(page_scores(b, slot), axis=1)
        m_prev = m_ref[...]
        m_new = jnp.maximum(m_prev, jnp.max(s, axis=1, keepdims=True))
        p = jnp.exp(s - m_new)
        alpha = jnp.exp(m_prev - m_new)
        l_ref[...] = alpha * l_ref[...] + jnp.sum(p, axis=1, keepdims=True)
        pb = p.astype(BF16)
        acc = alpha * acc_ref[...]
        for p_i in range(g):
            acc = acc + _dot_nt(pb[:, p_i * LANES:(p_i + 1) * LANES], vbuf[slot, p_i].astype(BF16))
        acc_ref[...] = acc
        m_ref[...] = m_new

    def write_row(b, l):
        o = acc_ref[...] / l
        r = lax.broadcasted_iota(jnp.int32, o.shape, 0)
        col = lax.broadcasted_iota(jnp.int32, o.shape, 1)
        o = jnp.where(col // head_dim == r % n_heads, o, 0.0)
        o_ref[b] = jnp.sum(o.reshape(seq, n_heads, o.shape[1]), axis=1)

    def finish_row(b, slot):
        write_row(b, l_ref[...] + jnp.sum(lsum_ref[...], axis=1, keepdims=True))
        excess = jnp.max(top_ref[...], axis=1, keepdims=True) - m_ref[...]

        @pl.when(jnp.max(excess) > SAMPLE_SLACK)
        def _():
            start_row(b)

            def redo(r, _):
                start_chunk(b * per_row + r, slot)
                wait_chunk(b * per_row + r, slot)
                online_chunk(b, slot)
                return 0

            lax.fori_loop(0, per_row, redo, 0)
            write_row(b, l_ref[...])

    @pl.when(t == 0)
    def _():
        for c0 in range(PAGE_SLOTS):
            start_chunk(c0, c0)

    for k in (0, 1):
        c = 2 * t + k
        slot = c % PAGE_SLOTS
        b, r = c // per_row, c % per_row

        @pl.when(r == 0)
        def _():
            start_row(b)

        wait_chunk(c, slot)
        one_pass_chunk(b, slot)

        if k == 1:
            @pl.when(r == per_row - 1)
            def _():
                finish_row(b, slot)

        @pl.when(c + PAGE_SLOTS < n_chunks)
        def _():
            start_chunk(c + PAGE_SLOTS, slot)


def _sample_specs(pt_flat, qbd, cq, knew, vnew, cnew, kt_pool, vt_pool, sfx_pool, *, g, n_pages, seq, n_heads,
                  head_dim):
    dec_batch, rows, width = qbd.shape
    page = kt_pool.shape[2]
    assert n_pages % (2 * g) == 0 and pt_flat.shape[0] == dec_batch * n_pages
    assert dec_batch * n_pages // g >= PAGE_SLOTS
    vmem = pl.BlockSpec(memory_space=pltpu.VMEM)
    hbm = pl.BlockSpec(memory_space=pl.ANY)
    in_specs = [pl.BlockSpec(memory_space=pltpu.SMEM)] + [vmem] * 5 + [hbm] * 3
    scratch = [
        pltpu.VMEM((PAGE_SLOTS, g, width, page), F32), pltpu.VMEM((PAGE_SLOTS, g, width, page), F32),
        pltpu.VMEM((PAGE_SLOTS, g, n_heads, 2 * page), F32), pltpu.SemaphoreType.DMA((3, PAGE_SLOTS)),
        pltpu.VMEM((rows, 1), F32), pltpu.VMEM((rows, 1), F32), pltpu.VMEM((rows, LANES), F32),
        pltpu.VMEM((rows, LANES), F32), pltpu.VMEM((rows, width), F32), pltpu.VMEM((n_heads, LANES), F32),
    ]
    return in_specs, jax.ShapeDtypeStruct((dec_batch, seq, width), F32), scratch, dec_batch * n_pages // (2 * g)


def kernel(x_prompt, x_sample, cache_k, cache_v, cache_logf, state_conv, page_table, g_ffn1, w1_ffn1, w3_ffn1,
           w2_ffn1, g_mix, w_in, b_f, conv_w, g_attn_out, g_conv_out, w_out, g_ffn2, w1_ffn2, w3_ffn2, w2_ffn2,
           g_final):
    batch, seq, d_model = x_prompt.shape
    dec_batch, dec_seq, _ = x_sample.shape
    depth, n_pool, page, n_heads, head_dim = cache_k.shape
    width = n_heads * head_dim
    assert depth == 1 and seq % ROW_TILE == 0 and page == LANES and 2 * head_dim == LANES
    assert w_in.shape[2] == 6 * width + n_heads

    xp = x_prompt.reshape(batch * seq, d_model)
    xs = x_sample.reshape(dec_batch * dec_seq, d_model)
    row = lambda a: a.reshape(1, -1)
    l = 0

    bw = lambda a: a.astype(BF16)
    wi = w_in[l]
    wf = jnp.tile(wi[:, 3 * width:3 * width + n_heads], (1, 3))
    w_proj = bw(jnp.concatenate([wi[:, :3 * width], wi[:, 3 * width + n_heads:],
                                 jnp.pad(wf, ((0, 0), (0, LANES - 3 * n_heads)))], axis=1))
    bf = jnp.pad(jnp.tile(b_f[l], 3), (0, LANES - 3 * n_heads)).reshape(1, LANES)
    ffn1 = (row(g_ffn1[l]), bw(w1_ffn1[l]), bw(w3_ffn1[l]), bw(w2_ffn1[l]))
    ffn2 = (row(g_ffn2[l]), bw(w1_ffn2[l]), bw(w3_ffn2[l]), bw(w2_ffn2[l]))
    wo = bw(w_out[l])
    merge_w = (row(g_attn_out[l]), wo[:width], wo[width:])
    gmix, cw, gco, gfin = row(g_mix[l]), conv_w[l], row(g_conv_out[l]), row(g_final)

    xs1 = _ffn(xs, *ffn1)
    st = state_conv[l]
    s1 = jnp.repeat(st[:, 1:2], dec_seq, axis=1).reshape(dec_batch * dec_seq, width)
    s2 = jnp.pad(st, ((0, 0), (0, dec_seq - 2), (0, 0))).reshape(dec_batch * dec_seq, width)
    q_s, k_s, v_s, logf_s, c_s, uc_s, cn_s = _proj_sample(
        xs1, gmix, w_proj, bf, cw, gco, s1, s2, seq=dec_seq, n_heads=n_heads, head_dim=head_dim)

    eye = jnp.eye(n_heads, dtype=F32)
    q4 = q_s.reshape(dec_batch, dec_seq, n_heads, 1, head_dim)
    qbd = bw((q4 * eye[None, None, :, :, None]).reshape(dec_batch, dec_seq * n_heads, width))
    c4 = c_s[:, :n_heads].reshape(dec_batch, dec_seq, n_heads)
    cq = jnp.broadcast_to(c4.reshape(dec_batch, dec_seq * n_heads, 1), (dec_batch, dec_seq * n_heads, LANES))
    n_new = 16
    cnew = jnp.broadcast_to(jnp.transpose(c4, (0, 2, 1))[:, None], (dec_batch, dec_seq, n_heads, dec_seq))
    cnew = jnp.pad(cnew.reshape(dec_batch, dec_seq * n_heads, dec_seq), ((0, 0), (0, 0), (0, n_new - dec_seq)))
    pad_new = lambda a: bw(jnp.pad(a.reshape(dec_batch, dec_seq, width), ((0, 0), (0, n_new - dec_seq), (0, 0))))
    kt_pool = jnp.transpose(cache_k[l], (0, 2, 3, 1)).reshape(n_pool, width, page)
    vt_pool = jnp.transpose(cache_v[l], (0, 2, 3, 1)).reshape(n_pool, width, page)
    sfx_pool = _suffix_pool(jnp.transpose(cache_logf[l], (0, 2, 1)))
    n_pages = page_table.shape[1]
    half = dec_batch // 2
    static = dict(g=PAGES_PER_STEP, n_pages=n_pages, seq=dec_seq, n_heads=n_heads, head_dim=head_dim)

    def sample_part(lo):
        rows = slice(lo, lo + half)
        per_row = (page_table[rows].reshape(-1), qbd[rows], cq[rows], pad_new(k_s)[rows], pad_new(v_s)[rows],
                   cnew[rows])
        return per_row + (kt_pool, vt_pool, sfx_pool), static

    xp1, o_s_lo = _ffn(xp, *ffn1, sample=sample_part(0))
    kt_p, vt_p, logft_p, qa, ka, va, cn_p, tail_p = _proj_prompt(
        xp1, gmix, w_proj, bf, cw, gco, batch=batch, n_heads=n_heads, head_dim=head_dim)
    o_p = _attn_prompt(qa, ka, va, head_dim=head_dim).reshape(batch * seq, width)
    y_p, o_s_hi = _ffn(xp1, *ffn2, merge=(o_p, cn_p) + merge_w, g_final=gfin, sample=sample_part(half))

    o_s = jnp.concatenate([o_s_lo, o_s_hi], axis=0).reshape(dec_batch * dec_seq, width)
    y_s = _ffn(xs1, *ffn2, merge=(o_s, cn_s) + merge_w, g_final=gfin)

    hd = (n_heads, head_dim)
    untranspose = lambda a: jnp.transpose(a.reshape(batch, n_heads, head_dim, seq), (0, 3, 1, 2))[None]
    return (
        y_p.reshape(batch, seq, d_model),
        y_s.reshape(dec_batch, dec_seq, d_model),
        untranspose(kt_p),
        untranspose(vt_p),
        jnp.transpose(logft_p, (0, 2, 1))[None],
        tail_p.reshape(1, batch, 2, width),
        k_s.reshape(1, dec_batch, dec_seq, *hd),
        v_s.reshape(1, dec_batch, dec_seq, *hd),
        logf_s.reshape(1, dec_batch, dec_seq, n_heads),
        uc_s.reshape(1, dec_batch, dec_seq, width)[:, :, dec_seq - 2:],
    )
```

```python
import functools
import math

import jax
import jax.numpy as jnp
from jax import lax
from jax.experimental import pallas as pl
from jax.experimental.pallas import tpu as pltpu

F32 = jnp.float32
BF16 = jnp.bfloat16
EPS = 1e-6
NEG = -1e30
LANES = 128
ROW_TILE = 512
SAMPLE_ROW_TILE = 256
PAGES_PER_STEP = 16
HEADS_PER_STEP = 8
SHIFT_SLACK = 64.0
COPY_UNROLL = 4
SAMPLE_SLACK = 40.0
SUFFIX_ROWS = 2048
LOG2E = 1.4426950408889634
VMEM_LIMIT = 56 * 1024 * 1024
AUG = 128
V_ROWS = 80


def _dot(a, b):
    return jnp.dot(a, b, preferred_element_type=F32)


def _dot_nt(a, b):
    return lax.dot_general(a, b, (((1,), (1,)), ((), ())), preferred_element_type=F32)


def _rms(x, g):
    return x * lax.rsqrt(jnp.mean(x * x, axis=-1, keepdims=True) + EPS) * g


def _split3(x):
    hi = x.astype(BF16).astype(F32)
    r1 = x - hi
    mid = r1.astype(BF16).astype(F32)
    lo = (r1 - mid).astype(BF16).astype(F32)
    return hi, mid, lo


def _const_spec(a):
    nd = a.ndim
    return pl.BlockSpec(a.shape, lambda *_: (0,) * nd, pipeline_mode=pl.Buffered(1))


def _ffn_body(*refs, merge, final, sample):
    x_ref, g_ref, w1_ref, w3_ref, w2_ref = refs[:5]
    rest = refs[5:]
    if merge:
        o_ref, cn_ref, gao_ref, woa_ref, woc_ref = rest[:5]
        rest = rest[5:]
    if final:
        gfin_ref = rest[0]
        rest = rest[1:]
    if sample:
        sample_in, rest = rest[:_N_SAMPLE_IN], rest[_N_SAMPLE_IN:]
        out_ref, os_ref = rest[:2]
        _sample_chunk_pair(pl.program_id(0), *sample_in, os_ref, *rest[2:], **sample)
    else:
        out_ref = rest[0]
    x = x_ref[...]
    if merge:
        an = _rms(o_ref[...], gao_ref[...]).astype(BF16)
        x = x + (_dot(an, woa_ref[...]) + _dot(cn_ref[...], woc_ref[...]))
    h = _rms(x, g_ref[...]).astype(BF16)
    a = _dot(h, w1_ref[...])
    b = _dot(h, w3_ref[...])
    act = (a * jax.nn.sigmoid(a) * b).astype(BF16)
    y = x + 0.5 * _dot(act, w2_ref[...])
    if final:
        y = _rms(y, gfin_ref[...])
    out_ref[...] = y


def _ffn(x, g, w1, w3, w2, *, merge=None, g_final=None, sample=None):
    n, d = x.shape
    tm = min(ROW_TILE if sample is None else SAMPLE_ROW_TILE, n)
    row = lambda i: (i, 0)
    args = [x, g, w1, w3, w2]
    specs = [pl.BlockSpec((tm, d), row)] + [_const_spec(a) for a in args[1:]]
    if merge is not None:
        o, cn, gao, woa, woc = merge
        args += [o, cn, gao, woa, woc]
        specs += [pl.BlockSpec((tm, o.shape[1]), row), pl.BlockSpec((tm, cn.shape[1]), row),
                  _const_spec(gao), _const_spec(woa), _const_spec(woc)]
    if g_final is not None:
        args.append(g_final)
        specs.append(_const_spec(g_final))
    out_shape = jax.ShapeDtypeStruct((n, d), F32)
    out_specs = pl.BlockSpec((tm, d), row)
    scratch, static = [], None
    if sample is not None:
        operands, static = sample
        s_specs, s_out_shape, scratch, n_steps = _sample_specs(*operands, **static)
        assert n // tm == n_steps and len(operands) == _N_SAMPLE_IN
        args += list(operands)
        specs += s_specs
        out_shape = (out_shape, s_out_shape)
        out_specs = (out_specs, pl.BlockSpec(memory_space=pltpu.VMEM))
    return pl.pallas_call(
        functools.partial(_ffn_body, merge=merge is not None, final=g_final is not None, sample=static),
        out_shape=out_shape,
        grid=(n // tm,),
        in_specs=specs,
        out_specs=out_specs,
        scratch_shapes=scratch,
        compiler_params=pltpu.CompilerParams(dimension_semantics=("arbitrary",), vmem_limit_bytes=VMEM_LIMIT),
        name=("ffn_merge" if merge is not None else "ffn") + ("_sample" if sample is not None else ""),
    )(*args)


def _proj_common(x_ref, g_ref, w_ref, bf_ref, tri_ref, width, scale):
    a = width
    h = _rms(x_ref[...], g_ref[...]).astype(BF16)
    p = _dot(h, w_ref[...])
    q = p[:, 0:a] * scale
    k = p[:, a:2 * a]
    v = p[:, 2 * a:3 * a]
    bg = p[:, 3 * a:4 * a]
    uc = p[:, 4 * a:5 * a] * p[:, 5 * a:6 * a]
    z = p[:, 6 * a:6 * a + LANES] + bf_ref[...]
    lane = lax.broadcasted_iota(jnp.int32, z.shape, 1)
    logf = jnp.minimum(z, 0.0) - jnp.log1p(jnp.exp(-jnp.abs(z)))
    l3 = jnp.where(lane < 24, logf, 0.0)
    hi, mid, lo = _split3(l3)
    packed = jnp.where(lane < 8, hi, jnp.where(lane < 16, mid, lo)).astype(BF16)
    r = _dot(tri_ref[...], packed)
    c = r + pltpu.roll(r, LANES - 8, axis=1) + pltpu.roll(r, LANES - 16, axis=1)
    c = jnp.where(lane < 8, c, 0.0)
    return q, k, v, bg, uc, logf, c, lane


def _conv_branch(uc, prev1, prev2, bg, cw_ref, gco_ref):
    y = cw_ref[0:1, :] * prev2 + cw_ref[1:2, :] * prev1 + cw_ref[2:3, :] * uc
    return _rms(bg * y, gco_ref[...]).astype(BF16)


def _proj_prompt_body(x_ref, g_ref, w_ref, bf_ref, tri_ref, cw_ref, gco_ref, psel_ref, asel_ref, hsel_ref,
                      kt_ref, vt_ref, logf_ref, qa_ref, ka_ref, va_ref, cn_ref, tail_ref,
                      carry_c, carry_u, *, n_heads, head_dim, scale):
    tm = x_ref.shape[0]
    width = n_heads * head_dim

    @pl.when(pl.program_id(1) == 0)
    def _():
        carry_c[...] = jnp.zeros_like(carry_c)
        carry_u[...] = jnp.zeros_like(carry_u)

    q, k, v, bg, uc, logf, c, lane = _proj_common(x_ref, g_ref, w_ref, bf_ref, tri_ref, width, scale * LOG2E)
    q_t = q.T
    v_t = v.T
    kt_ref[0] = k.T
    vt_ref[0] = v_t
    logf_ref[0] = logf.T[0:n_heads]

    c = c + carry_c[...]
    carry_c[...] = c[tm - 1:tm, :]
    c2 = c * LOG2E

    def packed_parts(x):
        rep = jnp.where(lane < 8, x, jnp.where(lane < 16, pltpu.roll(x, 8, axis=1),
                                               jnp.where(lane < 24, pltpu.roll(x, 16, axis=1), 0.0)))
        hi, mid, lo = _split3(rep)
        return jnp.where(lane < 8, hi, jnp.where(lane < 16, mid, jnp.where(lane < 24, lo,
                         jnp.where(lane == 24, 1.0, 0.0)))).astype(BF16)

    shift = _dot((q * k).astype(BF16), hsel_ref[...])
    k_extra = _dot(packed_parts(c2), psel_ref[...])
    q_extra = _dot_nt(asel_ref[...], packed_parts(c2 - shift))

    lane_a = lax.broadcasted_iota(jnp.int32, (tm, AUG), 1)
    row8 = lax.broadcasted_iota(jnp.int32, (8, tm), 0)
    ones_row = jnp.where(row8 == 0, 1.0, 0.0)
    pad_q = jnp.zeros((AUG - head_dim - 8, tm), F32)
    pad_v = jnp.zeros((V_ROWS - head_dim - 8, tm), F32)
    for h in range(n_heads):
        blk = k[:, (h // 2) * AUG:(h // 2 + 1) * AUG]
        if h % 2:
            blk = pltpu.roll(blk, head_dim, axis=1)
        ka_ref[0, h] = jnp.where(lane_a < head_dim, blk, k_extra[:, h * AUG:(h + 1) * AUG]).astype(BF16)
        qa_ref[0, h] = jnp.concatenate(
            [q_t[h * head_dim:(h + 1) * head_dim], q_extra[h * 8:(h + 1) * 8], pad_q], axis=0).astype(BF16)
        va_ref[0, h, 0] = jnp.concatenate(
            [v_t[h * head_dim:(h + 1) * head_dim], ones_row, pad_v], axis=0).astype(BF16)

    rows = lax.broadcasted_iota(jnp.int32, uc.shape, 0)
    last1 = carry_u[7:8, :]
    last2 = carry_u[6:7, :]
    prev1 = jnp.where(rows == 0, last1, pltpu.roll(uc, 1, axis=0))
    prev2 = jnp.where(rows == 0, last2, jnp.where(rows == 1, last1, pltpu.roll(uc, 2, axis=0)))
    cn_ref[...] = _conv_branch(uc, prev1, prev2, bg, cw_ref, gco_ref)
    tail_ref[0] = uc[tm - 2:tm, :]
    carry_u[...] = uc[tm - 8:tm, :]


def _proj_sample_body(x_ref, g_ref, w_ref, bf_ref, tri_ref, cw_ref, gco_ref, s1_ref, s2_ref,
                      q_ref, k_ref, v_ref, logf_ref, c_ref, uc_ref, cn_ref, *, n_heads, head_dim, scale, seq):
    q, k, v, bg, uc, logf, c, _ = _proj_common(x_ref, g_ref, w_ref, bf_ref, tri_ref, n_heads * head_dim, scale)
    q_ref[...] = q
    k_ref[...] = k
    v_ref[...] = v
    logf_ref[...] = logf[:, 0:n_heads]
    c_ref[...] = c
    uc_ref[...] = uc
    t = lax.broadcasted_iota(jnp.int32, uc.shape, 0) % seq
    prev1 = jnp.where(t == 0, s1_ref[...], pltpu.roll(uc, 1, axis=0))
    prev2 = jnp.where(t < 2, s2_ref[...], pltpu.roll(uc, 2, axis=0))
    cn_ref[...] = _conv_branch(uc, prev1, prev2, bg, cw_ref, gco_ref)


def _selectors(n_heads):
    psel = jnp.zeros((LANES, n_heads * AUG), F32)
    asel = jnp.zeros((n_heads * 8, LANES), F32)
    for h in range(n_heads):
        base = h * AUG + 64
        psel = psel.at[24, base:base + 3].set(1.0)
        for part in range(3):
            psel = psel.at[part * 8 + h, base + 3 + part].set(-1.0)
            asel = asel.at[h * 8 + part, part * 8 + h].set(1.0)
        asel = asel.at[h * 8 + 3:h * 8 + 6, 24].set(1.0)
    return psel.astype(BF16), asel.astype(BF16)


def _proj_prompt(x, g, w, bf, cw, gco, *, batch, n_heads, head_dim):
    n, d = x.shape
    seq = n // batch
    tm = ROW_TILE
    nb = seq // tm
    width = n_heads * head_dim
    tri = jnp.tril(jnp.ones((tm, tm), F32)).astype(BF16)
    psel, asel = _selectors(n_heads)
    hsel = (jnp.arange(width)[:, None] // head_dim == jnp.arange(LANES)[None, :]).astype(BF16)
    row = lambda b, i: (b * nb + i, 0)
    consts = [g, w, bf, tri, cw, gco, psel, asel, hsel]
    out_shape = (
        jax.ShapeDtypeStruct((batch, width, seq), F32),
        jax.ShapeDtypeStruct((batch, width, seq), F32),
        jax.ShapeDtypeStruct((batch, n_heads, seq), F32),
        jax.ShapeDtypeStruct((batch, n_heads, AUG, seq), BF16),
        jax.ShapeDtypeStruct((batch, n_heads, seq, AUG), BF16),
        jax.ShapeDtypeStruct((batch, n_heads, nb, V_ROWS, tm), BF16),
        jax.ShapeDtypeStruct((n, width), BF16),
        jax.ShapeDtypeStruct((batch, 2, width), F32),
    )
    out_specs = (
        pl.BlockSpec((1, width, tm), lambda b, i: (b, 0, i)),
        pl.BlockSpec((1, width, tm), lambda b, i: (b, 0, i)),
        pl.BlockSpec((1, n_heads, tm), lambda b, i: (b, 0, i)),
        pl.BlockSpec((1, n_heads, AUG, tm), lambda b, i: (b, 0, 0, i)),
        pl.BlockSpec((1, n_heads, tm, AUG), lambda b, i: (b, 0, i, 0)),
        pl.BlockSpec((1, n_heads, 1, V_ROWS, tm), lambda b, i: (b, 0, i, 0, 0)),
        pl.BlockSpec((tm, width), row),
        pl.BlockSpec((1, 2, width), lambda b, i: (b, 0, 0)),
    )
    return pl.pallas_call(
        functools.partial(_proj_prompt_body, n_heads=n_heads, head_dim=head_dim, scale=1.0 / math.sqrt(head_dim)),
        out_shape=out_shape,
        grid=(batch, nb),
        in_specs=[pl.BlockSpec((tm, d), row)] + [_const_spec(a) for a in consts],
        out_specs=out_specs,
        scratch_shapes=[pltpu.VMEM((1, LANES), F32), pltpu.VMEM((8, width), F32)],
        compiler_params=pltpu.CompilerParams(dimension_semantics=("arbitrary", "arbitrary"),
                                             vmem_limit_bytes=VMEM_LIMIT),
        name="proj_prompt",
    )(x, *consts)


def _proj_sample(x, g, w, bf, cw, gco, s1, s2, *, seq, n_heads, head_dim):
    n, d = x.shape
    width = n_heads * head_dim
    r = jnp.arange(n)
    tri = ((r[None, :] <= r[:, None]) & (r[None, :] // seq == r[:, None] // seq)).astype(BF16)
    args = [x, g, w, bf, tri, cw, gco, s1, s2]
    shapes = [(n, width)] * 3 + [(n, n_heads), (n, LANES), (n, width)]
    out_shape = tuple(jax.ShapeDtypeStruct(s, F32) for s in shapes) + (jax.ShapeDtypeStruct((n, width), BF16),)
    return pl.pallas_call(
        functools.partial(_proj_sample_body, n_heads=n_heads, head_dim=head_dim,
                          scale=1.0 / math.sqrt(head_dim), seq=seq),
        out_shape=out_shape,
        compiler_params=pltpu.CompilerParams(vmem_limit_bytes=VMEM_LIMIT),
        name="proj_sample",
    )(*args)


def _attn_prompt_body(qa_ref, ka_ref, va_ref, o_ref, acc_ref, s_ref, ot_ref, *, head_dim):
    i = pl.program_id(2)
    n_h = qa_ref.shape[1]
    tq = qa_ref.shape[3]
    tk = tq
    sub = lax.broadcasted_iota(jnp.int32, (tk, tq), 0)
    lan = lax.broadcasted_iota(jnp.int32, (tk, tq), 1)
    causal = sub <= lan

    def scores(j, hh):
        k_j = ka_ref[0, hh, pl.ds(pl.multiple_of(j * tk, tk), tk), :]
        return _dot(k_j, qa_ref[0, hh])

    def normalised(acc):
        return acc[0:head_dim] / acc[head_dim:head_dim + 1]

    top = []
    s_next = scores(i, 0)
    for hh in range(n_h):
        s = jnp.where(causal, s_next, NEG)
        s_next = scores(i, hh + 1) if hh + 1 < n_h else scores(0, 0)
        top.append(jnp.max(s, axis=0, keepdims=True))
        acc_ref[hh] = _dot(va_ref[0, hh, i], jnp.exp2(s).astype(BF16))
    s_ref[...] = s_next

    def one_pass_block(j, top):
        s_next = s_ref[...]
        new_top = []
        for hh in range(n_h):
            s = s_next
            s_next = scores(j, hh + 1) if hh + 1 < n_h else scores(j + 1, 0)
            new_top.append(jnp.maximum(top[hh], jnp.max(s, axis=0, keepdims=True)))
            acc_ref[hh] += _dot(va_ref[0, hh, j], jnp.exp2(s).astype(BF16))
        s_ref[...] = s_next
        return tuple(new_top)

    top = lax.fori_loop(0, i, one_pass_block, tuple(top))
    o_ref[0] = jnp.concatenate([normalised(acc_ref[hh]) for hh in range(n_h)], axis=0).T
    hi, lo = top[0], top[0]
    for t in top[1:]:
        hi, lo = jnp.maximum(hi, t), jnp.minimum(lo, t)

    @pl.when((jnp.max(hi) > SHIFT_SLACK) | (jnp.min(lo) < -SHIFT_SLACK))
    def _():
        def online_head(hh, _):
            q_t = qa_ref[0, hh]

            def sc(j):
                return _dot(ka_ref[0, hh, pl.ds(pl.multiple_of(j * tk, tk), tk), :], q_t)

            s = jnp.where(causal, sc(i), NEG)
            m = jnp.max(s, axis=0, keepdims=True)
            acc = _dot(va_ref[0, hh, i], jnp.exp2(s - m).astype(BF16))

            def body(j, carry):
                m, acc = carry
                s = sc(j)
                m_new = jnp.maximum(m, jnp.max(s, axis=0, keepdims=True))
                p = jnp.exp2(s - m_new).astype(BF16)
                return m_new, jnp.exp2(m - m_new) * acc + _dot(va_ref[0, hh, j], p)

            m, acc = lax.fori_loop(0, i, body, (m, acc))
            ot_ref[pl.ds(pl.multiple_of(hh * head_dim, head_dim), head_dim), :] = normalised(acc)
            return 0

        lax.fori_loop(0, n_h, online_head, 0)
        o_ref[0] = ot_ref[...].T


def _attn_prompt(qa, ka, va, *, head_dim):
    batch, n_heads, _, seq = qa.shape
    nb, tq = va.shape[2], va.shape[4]
    hps = HEADS_PER_STEP
    resident = dict(pipeline_mode=pl.Buffered(1))
    return pl.pallas_call(
        functools.partial(_attn_prompt_body, head_dim=head_dim),
        out_shape=jax.ShapeDtypeStruct((batch, seq, n_heads * head_dim), F32),
        grid=(batch, n_heads // hps, nb),
        in_specs=[
            pl.BlockSpec((1, hps, AUG, tq), lambda b, g, i: (b, g, 0, i)),
            pl.BlockSpec((1, hps, seq, AUG), lambda b, g, i: (b, g, 0, 0), **resident),
            pl.BlockSpec((1, hps, nb, V_ROWS, tq), lambda b, g, i: (b, g, 0, 0, 0), **resident),
        ],
        out_specs=pl.BlockSpec((1, tq, hps * head_dim), lambda b, g, i: (b, i, g)),
        scratch_shapes=[pltpu.VMEM((hps, V_ROWS, tq), F32), pltpu.VMEM((tq, tq), F32),
                        pltpu.VMEM((hps * head_dim, tq), F32)],
        compiler_params=pltpu.CompilerParams(dimension_semantics=("arbitrary",) * 3, vmem_limit_bytes=VMEM_LIMIT),
        name="attn_prompt",
    )(qa, ka, va)


def _suffix_body(x_ref, rhs_ref, o_ref):
    hi, mid, lo = _split3(x_ref[...])
    rhs = rhs_ref[...]
    o_ref[...] = _dot(hi.astype(BF16), rhs) + _dot(mid.astype(BF16), rhs) + _dot(lo.astype(BF16), rhs)


def _suffix_pool(lf_pool):
    n_pool, n_heads, page = lf_pool.shape
    rows = n_pool * n_heads
    tr = SUFFIX_ROWS
    assert rows % tr == 0
    r = jnp.arange(page)
    rhs = jnp.concatenate([(r[:, None] > r[None, :]).astype(BF16), jnp.ones((page, page), BF16)], axis=1)
    out = pl.pallas_call(
        _suffix_body,
        out_shape=jax.ShapeDtypeStruct((rows, 2 * page), F32),
        grid=(rows // tr,),
        in_specs=[pl.BlockSpec((tr, page), lambda i: (i, 0)), _const_spec(rhs)],
        out_specs=pl.BlockSpec((tr, 2 * page), lambda i: (i, 0)),
        compiler_params=pltpu.CompilerParams(dimension_semantics=("arbitrary",), vmem_limit_bytes=VMEM_LIMIT),
        name="suffix_pool",
    )(lf_pool.reshape(rows, page), rhs)
    return out.reshape(n_pool, n_heads, 2 * page)


_N_SAMPLE_IN = 9


def _sample_chunk_pair(t, pt_ref, qbd_ref, cq_ref, knew_ref, vnew_ref, cnew_ref, kt_hbm, vt_hbm, sfx_hbm, o_ref,
                       kbuf, vbuf, sbuf, sem, m_ref, l_ref, lsum_ref, top_ref, acc_ref, carry_ref,
                       *, g, n_pages, seq, n_heads, head_dim):
    dec_batch = qbd_ref.shape[0]
    n_new = knew_ref.shape[1]
    per_row = n_pages // g
    n_chunks = dec_batch * per_row

    def first_page(c):
        return (c // per_row) * n_pages + (n_pages - 1) - (c % per_row) * g

    def page_copies(first, slot, p_i):
        page = pt_ref[first - p_i]
        return (pltpu.make_async_copy(kt_hbm.at[page], kbuf.at[slot, p_i], sem.at[0, slot]),
                pltpu.make_async_copy(vt_hbm.at[page], vbuf.at[slot, p_i], sem.at[1, slot]),
                pltpu.make_async_copy(sfx_hbm.at[page], sbuf.at[slot, p_i], sem.at[2, slot]))

    def start_chunk(c, slot):
        first = first_page(c)

        def body(p_i, _):
            for cp in page_copies(first, slot, p_i):
                cp.start()
            return 0
        lax.fori_loop(0, g, body, 0, unroll=COPY_UNROLL)

    def wait_chunk(c, slot):
        first = first_page(c)

        def body(p_i, _):
            for cp in page_copies(first, slot, p_i):
                cp.wait()
            return 0
        lax.fori_loop(0, g, body, 0, unroll=COPY_UNROLL)

    def start_row(b):
        qb, cq = qbd_ref[b], cq_ref[b]
        s = _dot_nt(qb, knew_ref[b]) + cq[:, 0:n_new] - cnew_ref[b]
        t_row = lax.broadcasted_iota(jnp.int32, s.shape, 0) // n_heads
        j = lax.broadcasted_iota(jnp.int32, s.shape, 1)
        s = jnp.where((j <= t_row) & (j < seq), s, NEG)
        m = jnp.max(s, axis=1, keepdims=True)
        p = jnp.exp(s - m)
        m_ref[...] = m
        l_ref[...] = jnp.sum(p, axis=1, keepdims=True)
        acc_ref[...] = _dot(p.astype(BF16), vnew_ref[b])
        lsum_ref[...] = jnp.zeros_like(lsum_ref)
        top_ref[...] = jnp.full_like(top_ref, NEG)
        carry_ref[...] = jnp.zeros_like(carry_ref)

    def page_scores(b, slot):
        qb, cq = qbd_ref[b], cq_ref[b]
        carry = carry_ref[...]
        parts = []
        for p_i in range(g):
            sfx = sbuf[slot, p_i]
            suffix = sfx[:, 0:LANES] + carry
            carry = carry + sfx[:, LANES:2 * LANES]
            bias = jnp.concatenate([suffix] * seq, axis=0) + cq
            parts.append(_dot(qb, kbuf[slot, p_i].astype(BF16)) + bias)
        carry_ref[...] = carry
        return parts

    def one_pass_chunk(b, slot):
        m = m_ref[...]
        top, lsum, acc = top_ref[...], lsum_ref[...], acc_ref[...]
        for p_i, s in enumerate(page_scores(b, slot)):
            top = jnp.maximum(top, s)
            p = jnp.exp(s - m)
            lsum = lsum + p
            acc = acc + _dot_nt(p.astype(BF16), vbuf[slot, p_i].astype(BF16))
        top_ref[...], lsum_ref[...], acc_ref[...] = top, lsum, acc

    def online_chunk(b, slot):
        s = jnp.concatenate(page_scores(b, slot), axis=1)
        m_prev = m_ref[...]
        m_new = jnp.maximum(m_prev, jnp.max(s, axis=1, keepdims=True))
        p = jnp.exp(s - m_new)
        alpha = jnp.exp(m_prev - m_new)
        l_ref[...] = alpha * l_ref[...] + jnp.sum(p, axis=1, keepdims=True)
        pb = p.astype(BF16)
        acc = alpha * acc_ref[...]
        for p_i in range(g):
            acc = acc + _dot_nt(pb[:, p_i * LANES:(p_i + 1) * LANES], vbuf[slot, p_i].astype(BF16))
        acc_ref[...] = acc
        m_ref[...] = m_new

    def write_row(b, l):
        o = acc_ref[...] / l
        r = lax.broadcasted_iota(jnp.int32, o.shape, 0)
        col = lax.broadcasted_iota(jnp.int32, o.shape, 1)
        o = jnp.where(col // head_dim == r % n_heads, o, 0.0)
        o_ref[b] = jnp.sum(o.reshape(seq, n_heads, o.shape[1]), axis=1)

    def finish_row(b, slot):
        write_row(b, l_ref[...] + jnp.sum(lsum_ref[...], axis=1, keepdims=True))
        excess = jnp.max(top_ref[...], axis=1, keepdims=True) - m_ref[...]

        @pl.when(jnp.max(excess) > SAMPLE_SLACK)
        def _():
            start_row(b)

            def redo(r, _):
                start_chunk(b * per_row + r, slot)
                wait_chunk(b * per_row + r, slot)
                online_chunk(b, slot)
                return 0

            lax.fori_loop(0, per_row, redo, 0)
            write_row(b, l_ref[...])

    @pl.when(t == 0)
    def _():
        start_chunk(0, 0)
        start_chunk(1, 1)

    for slot in (0, 1):
        c = 2 * t + slot
        b, r = c // per_row, c % per_row

        @pl.when(r == 0)
        def _():
            start_row(b)

        wait_chunk(c, slot)
        one_pass_chunk(b, slot)

        if slot == 1:
            @pl.when(r == per_row - 1)
            def _():
                finish_row(b, slot)

        @pl.when(c + 2 < n_chunks)
        def _():
            start_chunk(c + 2, slot)


def _sample_specs(pt_flat, qbd, cq, knew, vnew, cnew, kt_pool, vt_pool, sfx_pool, *, g, n_pages, seq, n_heads,
                  head_dim):
    dec_batch, rows, width = qbd.shape
    page = kt_pool.shape[2]
    assert n_pages % (2 * g) == 0 and pt_flat.shape[0] == dec_batch * n_pages
    vmem = pl.BlockSpec(memory_space=pltpu.VMEM)
    hbm = pl.BlockSpec(memory_space=pl.ANY)
    in_specs = [pl.BlockSpec(memory_space=pltpu.SMEM)] + [vmem] * 5 + [hbm] * 3
    scratch = [
        pltpu.VMEM((2, g, width, page), F32), pltpu.VMEM((2, g, width, page), F32),
        pltpu.VMEM((2, g, n_heads, 2 * page), F32), pltpu.SemaphoreType.DMA((3, 2)),
        pltpu.VMEM((rows, 1), F32), pltpu.VMEM((rows, 1), F32), pltpu.VMEM((rows, LANES), F32),
        pltpu.VMEM((rows, LANES), F32), pltpu.VMEM((rows, width), F32), pltpu.VMEM((n_heads, LANES), F32),
    ]
    return in_specs, jax.ShapeDtypeStruct((dec_batch, seq, width), F32), scratch, dec_batch * n_pages // (2 * g)


def kernel(x_prompt, x_sample, cache_k, cache_v, cache_logf, state_conv, page_table, g_ffn1, w1_ffn1, w3_ffn1,
           w2_ffn1, g_mix, w_in, b_f, conv_w, g_attn_out, g_conv_out, w_out, g_ffn2, w1_ffn2, w3_ffn2, w2_ffn2,
           g_final):
    batch, seq, d_model = x_prompt.shape
    dec_batch, dec_seq, _ = x_sample.shape
    depth, n_pool, page, n_heads, head_dim = cache_k.shape
    width = n_heads * head_dim
    assert depth == 1 and seq % ROW_TILE == 0 and page == LANES and 2 * head_dim == LANES
    assert w_in.shape[2] == 6 * width + n_heads

    xp = x_prompt.reshape(batch * seq, d_model)
    xs = x_sample.reshape(dec_batch * dec_seq, d_model)
    row = lambda a: a.reshape(1, -1)
    l = 0

    bw = lambda a: a.astype(BF16)
    wi = w_in[l]
    wf = jnp.tile(wi[:, 3 * width:3 * width + n_heads], (1, 3))
    w_proj = bw(jnp.concatenate([wi[:, :3 * width], wi[:, 3 * width + n_heads:],
                                 jnp.pad(wf, ((0, 0), (0, LANES - 3 * n_heads)))], axis=1))
    bf = jnp.pad(jnp.tile(b_f[l], 3), (0, LANES - 3 * n_heads)).reshape(1, LANES)
    ffn1 = (row(g_ffn1[l]), bw(w1_ffn1[l]), bw(w3_ffn1[l]), bw(w2_ffn1[l]))
    ffn2 = (row(g_ffn2[l]), bw(w1_ffn2[l]), bw(w3_ffn2[l]), bw(w2_ffn2[l]))
    wo = bw(w_out[l])
    merge_w = (row(g_attn_out[l]), wo[:width], wo[width:])
    gmix, cw, gco, gfin = row(g_mix[l]), conv_w[l], row(g_conv_out[l]), row(g_final)

    xs1 = _ffn(xs, *ffn1)
    st = state_conv[l]
    s1 = jnp.repeat(st[:, 1:2], dec_seq, axis=1).reshape(dec_batch * dec_seq, width)
    s2 = jnp.pad(st, ((0, 0), (0, dec_seq - 2), (0, 0))).reshape(dec_batch * dec_seq, width)
    q_s, k_s, v_s, logf_s, c_s, uc_s, cn_s = _proj_sample(
        xs1, gmix, w_proj, bf, cw, gco, s1, s2, seq=dec_seq, n_heads=n_heads, head_dim=head_dim)

    eye = jnp.eye(n_heads, dtype=F32)
    q4 = q_s.reshape(dec_batch, dec_seq, n_heads, 1, head_dim)
    qbd = bw((q4 * eye[None, None, :, :, None]).reshape(dec_batch, dec_seq * n_heads, width))
    c4 = c_s[:, :n_heads].reshape(dec_batch, dec_seq, n_heads)
    cq = jnp.broadcast_to(c4.reshape(dec_batch, dec_seq * n_heads, 1), (dec_batch, dec_seq * n_heads, LANES))
    n_new = 16
    cnew = jnp.broadcast_to(jnp.transpose(c4, (0, 2, 1))[:, None], (dec_batch, dec_seq, n_heads, dec_seq))
    cnew = jnp.pad(cnew.reshape(dec_batch, dec_seq * n_heads, dec_seq), ((0, 0), (0, 0), (0, n_new - dec_seq)))
    pad_new = lambda a: bw(jnp.pad(a.reshape(dec_batch, dec_seq, width), ((0, 0), (0, n_new - dec_seq), (0, 0))))
    kt_pool = jnp.transpose(cache_k[l], (0, 2, 3, 1)).reshape(n_pool, width, page)
    vt_pool = jnp.transpose(cache_v[l], (0, 2, 3, 1)).reshape(n_pool, width, page)
    sfx_pool = _suffix_pool(jnp.transpose(cache_logf[l], (0, 2, 1)))
    n_pages = page_table.shape[1]
    half = dec_batch // 2
    static = dict(g=PAGES_PER_STEP, n_pages=n_pages, seq=dec_seq, n_heads=n_heads, head_dim=head_dim)

    def sample_part(lo):
        rows = slice(lo, lo + half)
        per_row = (page_table[rows].reshape(-1), qbd[rows], cq[rows], pad_new(k_s)[rows], pad_new(v_s)[rows],
                   cnew[rows])
        return per_row + (kt_pool, vt_pool, sfx_pool), static

    xp1, o_s_lo = _ffn(xp, *ffn1, sample=sample_part(0))
    kt_p, vt_p, logft_p, qa, ka, va, cn_p, tail_p = _proj_prompt(
        xp1, gmix, w_proj, bf, cw, gco, batch=batch, n_heads=n_heads, head_dim=head_dim)
    o_p = _attn_prompt(qa, ka, va, head_dim=head_dim).reshape(batch * seq, width)
    y_p, o_s_hi = _ffn(xp1, *ffn2, merge=(o_p, cn_p) + merge_w, g_final=gfin, sample=sample_part(half))

    o_s = jnp.concatenate([o_s_lo, o_s_hi], axis=0).reshape(dec_batch * dec_seq, width)
    y_s = _ffn(xs1, *ffn2, merge=(o_s, cn_s) + merge_w, g_final=gfin)

    hd = (n_heads, head_dim)
    untranspose = lambda a: jnp.transpose(a.reshape(batch, n_heads, head_dim, seq), (0, 3, 1, 2))[None]
    return (
        y_p.reshape(batch, seq, d_model),
        y_s.reshape(dec_batch, dec_seq, d_model),
        untranspose(kt_p),
        untranspose(vt_p),
        jnp.transpose(logft_p, (0, 2, 1))[None],
        tail_p.reshape(1, batch, 2, width),
        k_s.reshape(1, dec_batch, dec_seq, *hd),
        v_s.reshape(1, dec_batch, dec_seq, *hd),
        logf_s.reshape(1, dec_batch, dec_seq, n_heads),
        uc_s.reshape(1, dec_batch, dec_seq, width)[:, :, dec_seq - 2:],
    )
```

```python
import functools
import math

import jax
import jax.numpy as jnp
from jax import lax
from jax.experimental import pallas as pl
from jax.experimental.pallas import tpu as pltpu

F32 = jnp.float32
BF16 = jnp.bfloat16
EPS = 1e-6
NEG = -1e30
LANES = 128
ROW_TILE = 512
SAMPLE_ROW_TILE = 256
PAGES_PER_STEP = 16
HEADS_PER_STEP = 8
SHIFT_SLACK = 64.0
COPY_UNROLL = 4
SAMPLE_SLACK = 40.0
SUFFIX_ROWS = 2048
LOG2E = 1.4426950408889634
VMEM_LIMIT = 56 * 1024 * 1024
AUG = 128
V_ROWS = 80


def _dot(a, b):
    return jnp.dot(a, b, preferred_element_type=F32)


def _dot_nt(a, b):
    return lax.dot_general(a, b, (((1,), (1,)), ((), ())), preferred_element_type=F32)


def _rms(x, g):
    return x * lax.rsqrt(jnp.mean(x * x, axis=-1, keepdims=True) + EPS) * g


def _split3(x):
    hi = x.astype(BF16).astype(F32)
    r1 = x - hi
    mid = r1.astype(BF16).astype(F32)
    lo = (r1 - mid).astype(BF16).astype(F32)
    return hi, mid, lo


def _const_spec(a):
    nd = a.ndim
    return pl.BlockSpec(a.shape, lambda *_: (0,) * nd, pipeline_mode=pl.Buffered(1))


def _ffn_body(*refs, merge, final, sample):
    x_ref, g_ref, w1_ref, w3_ref, w2_ref = refs[:5]
    rest = refs[5:]
    if merge:
        o_ref, cn_ref, gao_ref, woa_ref, woc_ref = rest[:5]
        rest = rest[5:]
    if final:
        gfin_ref = rest[0]
        rest = rest[1:]
    if sample:
        sample_in, rest = rest[:_N_SAMPLE_IN], rest[_N_SAMPLE_IN:]
        out_ref, os_ref = rest[:2]
        _sample_chunk_pair(pl.program_id(0), *sample_in, os_ref, *rest[2:], **sample)
    else:
        out_ref = rest[0]
    x = x_ref[...]
    if merge:
        an = _rms(o_ref[...], gao_ref[...]).astype(BF16)
        x = x + (_dot(an, woa_ref[...]) + _dot(cn_ref[...], woc_ref[...]))
    h = _rms(x, g_ref[...]).astype(BF16)
    a = _dot(h, w1_ref[...])
    b = _dot(h, w3_ref[...])
    act = (a * jax.nn.sigmoid(a) * b).astype(BF16)
    y = x + 0.5 * _dot(act, w2_ref[...])
    if final:
        y = _rms(y, gfin_ref[...])
    out_ref[...] = y


def _ffn(x, g, w1, w3, w2, *, merge=None, g_final=None, sample=None):
    n, d = x.shape
    tm = min(ROW_TILE if sample is None else SAMPLE_ROW_TILE, n)
    row = lambda i: (i, 0)
    args = [x, g, w1, w3, w2]
    specs = [pl.BlockSpec((tm, d), row)] + [_const_spec(a) for a in args[1:]]
    if merge is not None:
        o, cn, gao, woa, woc = merge
        args += [o, cn, gao, woa, woc]
        specs += [pl.BlockSpec((tm, o.shape[1]), row), pl.BlockSpec((tm, cn.shape[1]), row),
                  _const_spec(gao), _const_spec(woa), _const_spec(woc)]
    if g_final is not None:
        args.append(g_final)
        specs.append(_const_spec(g_final))
    out_shape = jax.ShapeDtypeStruct((n, d), F32)
    out_specs = pl.BlockSpec((tm, d), row)
    scratch, static = [], None
    if sample is not None:
        operands, static = sample
        s_specs, s_out_shape, scratch, n_steps = _sample_specs(*operands, **static)
        assert n // tm == n_steps and len(operands) == _N_SAMPLE_IN
        args += list(operands)
        specs += s_specs
        out_shape = (out_shape, s_out_shape)
        out_specs = (out_specs, pl.BlockSpec(memory_space=pltpu.VMEM))
    return pl.pallas_call(
        functools.partial(_ffn_body, merge=merge is not None, final=g_final is not None, sample=static),
        out_shape=out_shape,
        grid=(n // tm,),
        in_specs=specs,
        out_specs=out_specs,
        scratch_shapes=scratch,
        compiler_params=pltpu.CompilerParams(dimension_semantics=("arbitrary",), vmem_limit_bytes=VMEM_LIMIT),
        name=("ffn_merge" if merge is not None else "ffn") + ("_sample" if sample is not None else ""),
    )(*args)


def _proj_common(x_ref, g_ref, w_ref, bf_ref, tri_ref, width, scale):
    a = width
    h = _rms(x_ref[...], g_ref[...]).astype(BF16)
    p = _dot(h, w_ref[...])
    q = p[:, 0:a] * scale
    k = p[:, a:2 * a]
    v = p[:, 2 * a:3 * a]
    bg = p[:, 3 * a:4 * a]
    uc = p[:, 4 * a:5 * a] * p[:, 5 * a:6 * a]
    z = p[:, 6 * a:6 * a + LANES] + bf_ref[...]
    lane = lax.broadcasted_iota(jnp.int32, z.shape, 1)
    logf = jnp.minimum(z, 0.0) - jnp.log1p(jnp.exp(-jnp.abs(z)))
    l3 = jnp.where(lane < 24, logf, 0.0)
    hi, mid, lo = _split3(l3)
    packed = jnp.where(lane < 8, hi, jnp.where(lane < 16, mid, lo)).astype(BF16)
    r = _dot(tri_ref[...], packed)
    c = r + pltpu.roll(r, LANES - 8, axis=1) + pltpu.roll(r, LANES - 16, axis=1)
    c = jnp.where(lane < 8, c, 0.0)
    return q, k, v, bg, uc, logf, c, lane


def _conv_branch(uc, prev1, prev2, bg, cw_ref, gco_ref):
    y = cw_ref[0:1, :] * prev2 + cw_ref[1:2, :] * prev1 + cw_ref[2:3, :] * uc
    return _rms(bg * y, gco_ref[...]).astype(BF16)


def _proj_prompt_body(x_ref, g_ref, w_ref, bf_ref, tri_ref, cw_ref, gco_ref, psel_ref, asel_ref, hsel_ref,
                      kt_ref, vt_ref, logf_ref, qa_ref, ka_ref, va_ref, cn_ref, tail_ref,
                      carry_c, carry_u, *, n_heads, head_dim, scale):
    tm = x_ref.shape[0]
    width = n_heads * head_dim

    @pl.when(pl.program_id(1) == 0)
    def _():
        carry_c[...] = jnp.zeros_like(carry_c)
        carry_u[...] = jnp.zeros_like(carry_u)

    q, k, v, bg, uc, logf, c, lane = _proj_common(x_ref, g_ref, w_ref, bf_ref, tri_ref, width, scale * LOG2E)
    q_t = q.T
    v_t = v.T
    kt_ref[0] = k.T
    vt_ref[0] = v_t
    logf_ref[0] = logf.T[0:n_heads]

    c = c + carry_c[...]
    carry_c[...] = c[tm - 1:tm, :]
    c2 = c * LOG2E

    def packed_parts(x):
        rep = jnp.where(lane < 8, x, jnp.where(lane < 16, pltpu.roll(x, 8, axis=1),
                                               jnp.where(lane < 24, pltpu.roll(x, 16, axis=1), 0.0)))
        hi, mid, lo = _split3(rep)
        return jnp.where(lane < 8, hi, jnp.where(lane < 16, mid, jnp.where(lane < 24, lo,
                         jnp.where(lane == 24, 1.0, 0.0)))).astype(BF16)

    shift = _dot((q * k).astype(BF16), hsel_ref[...])
    k_extra = _dot(packed_parts(c2), psel_ref[...])
    q_extra = _dot_nt(asel_ref[...], packed_parts(c2 - shift))

    lane_a = lax.broadcasted_iota(jnp.int32, (tm, AUG), 1)
    row8 = lax.broadcasted_iota(jnp.int32, (8, tm), 0)
    ones_row = jnp.where(row8 == 0, 1.0, 0.0)
    pad_q = jnp.zeros((AUG - head_dim - 8, tm), F32)
    pad_v = jnp.zeros((V_ROWS - head_dim - 8, tm), F32)
    for h in range(n_heads):
        blk = k[:, (h // 2) * AUG:(h // 2 + 1) * AUG]
        if h % 2:
            blk = pltpu.roll(blk, head_dim, axis=1)
        ka_ref[0, h] = jnp.where(lane_a < head_dim, blk, k_extra[:, h * AUG:(h + 1) * AUG]).astype(BF16)
        qa_ref[0, h] = jnp.concatenate(
            [q_t[h * head_dim:(h + 1) * head_dim], q_extra[h * 8:(h + 1) * 8], pad_q], axis=0).astype(BF16)
        va_ref[0, h, 0] = jnp.concatenate(
            [v_t[h * head_dim:(h + 1) * head_dim], ones_row, pad_v], axis=0).astype(BF16)

    rows = lax.broadcasted_iota(jnp.int32, uc.shape, 0)
    last1 = carry_u[7:8, :]
    last2 = carry_u[6:7, :]
    prev1 = jnp.where(rows == 0, last1, pltpu.roll(uc, 1, axis=0))
    prev2 = jnp.where(rows == 0, last2, jnp.where(rows == 1, last1, pltpu.roll(uc, 2, axis=0)))
    cn_ref[...] = _conv_branch(uc, prev1, prev2, bg, cw_ref, gco_ref)
    tail_ref[0] = uc[tm - 2:tm, :]
    carry_u[...] = uc[tm - 8:tm, :]


def _proj_sample_body(x_ref, g_ref, w_ref, bf_ref, tri_ref, cw_ref, gco_ref, s1_ref, s2_ref,
                      q_ref, k_ref, v_ref, logf_ref, c_ref, uc_ref, cn_ref, *, n_heads, head_dim, scale, seq):
    q, k, v, bg, uc, logf, c, _ = _proj_common(x_ref, g_ref, w_ref, bf_ref, tri_ref, n_heads * head_dim, scale)
    q_ref[...] = q
    k_ref[...] = k
    v_ref[...] = v
    logf_ref[...] = logf[:, 0:n_heads]
    c_ref[...] = c
    uc_ref[...] = uc
    t = lax.broadcasted_iota(jnp.int32, uc.shape, 0) % seq
    prev1 = jnp.where(t == 0, s1_ref[...], pltpu.roll(uc, 1, axis=0))
    prev2 = jnp.where(t < 2, s2_ref[...], pltpu.roll(uc, 2, axis=0))
    cn_ref[...] = _conv_branch(uc, prev1, prev2, bg, cw_ref, gco_ref)


def _selectors(n_heads):
    psel = jnp.zeros((LANES, n_heads * AUG), F32)
    asel = jnp.zeros((n_heads * 8, LANES), F32)
    for h in range(n_heads):
        base = h * AUG + 64
        psel = psel.at[24, base:base + 3].set(1.0)
        for part in range(3):
            psel = psel.at[part * 8 + h, base + 3 + part].set(-1.0)
            asel = asel.at[h * 8 + part, part * 8 + h].set(1.0)
        asel = asel.at[h * 8 + 3:h * 8 + 6, 24].set(1.0)
    return psel.astype(BF16), asel.astype(BF16)


def _proj_prompt(x, g, w, bf, cw, gco, *, batch, n_heads, head_dim):
    n, d = x.shape
    seq = n // batch
    tm = ROW_TILE
    nb = seq // tm
    width = n_heads * head_dim
    tri = jnp.tril(jnp.ones((tm, tm), F32)).astype(BF16)
    psel, asel = _selectors(n_heads)
    hsel = (jnp.arange(width)[:, None] // head_dim == jnp.arange(LANES)[None, :]).astype(BF16)
    row = lambda b, i: (b * nb + i, 0)
    consts = [g, w, bf, tri, cw, gco, psel, asel, hsel]
    out_shape = (
        jax.ShapeDtypeStruct((batch, width, seq), F32),
        jax.ShapeDtypeStruct((batch, width, seq), F32),
        jax.ShapeDtypeStruct((batch, n_heads, seq), F32),
        jax.ShapeDtypeStruct((batch, n_heads, AUG, seq), BF16),
        jax.ShapeDtypeStruct((batch, n_heads, seq, AUG), BF16),
        jax.ShapeDtypeStruct((batch, n_heads, nb, V_ROWS, tm), BF16),
        jax.ShapeDtypeStruct((n, width), BF16),
        jax.ShapeDtypeStruct((batch, 2, width), F32),
    )
    out_specs = (
        pl.BlockSpec((1, width, tm), lambda b, i: (b, 0, i)),
        pl.BlockSpec((1, width, tm), lambda b, i: (b, 0, i)),
        pl.BlockSpec((1, n_heads, tm), lambda b, i: (b, 0, i)),
        pl.BlockSpec((1, n_heads, AUG, tm), lambda b, i: (b, 0, 0, i)),
        pl.BlockSpec((1, n_heads, tm, AUG), lambda b, i: (b, 0, i, 0)),
        pl.BlockSpec((1, n_heads, 1, V_ROWS, tm), lambda b, i: (b, 0, i, 0, 0)),
        pl.BlockSpec((tm, width), row),
        pl.BlockSpec((1, 2, width), lambda b, i: (b, 0, 0)),
    )
    return pl.pallas_call(
        functools.partial(_proj_prompt_body, n_heads=n_heads, head_dim=head_dim, scale=1.0 / math.sqrt(head_dim)),
        out_shape=out_shape,
        grid=(batch, nb),
        in_specs=[pl.BlockSpec((tm, d), row)] + [_const_spec(a) for a in consts],
        out_specs=out_specs,
        scratch_shapes=[pltpu.VMEM((1, LANES), F32), pltpu.VMEM((8, width), F32)],
        compiler_params=pltpu.CompilerParams(dimension_semantics=("arbitrary", "arbitrary"),
                                             vmem_limit_bytes=VMEM_LIMIT),
        name="proj_prompt",
    )(x, *consts)


def _proj_sample(x, g, w, bf, cw, gco, s1, s2, *, seq, n_heads, head_dim):
    n, d = x.shape
    width = n_heads * head_dim
    r = jnp.arange(n)
    tri = ((r[None, :] <= r[:, None]) & (r[None, :] // seq == r[:, None] // seq)).astype(BF16)
    args = [x, g, w, bf, tri, cw, gco, s1, s2]
    shapes = [(n, width)] * 3 + [(n, n_heads), (n, LANES), (n, width)]
    out_shape = tuple(jax.ShapeDtypeStruct(s, F32) for s in shapes) + (jax.ShapeDtypeStruct((n, width), BF16),)
    return pl.pallas_call(
        functools.partial(_proj_sample_body, n_heads=n_heads, head_dim=head_dim,
                          scale=1.0 / math.sqrt(head_dim), seq=seq),
        out_shape=out_shape,
        compiler_params=pltpu.CompilerParams(vmem_limit_bytes=VMEM_LIMIT),
        name="proj_sample",
    )(*args)


def _attn_prompt_body(qa_ref, ka_ref, va_ref, o_ref, acc_ref, top_ref, s_ref, ot_ref, *, head_dim):
    i = pl.program_id(2)
    n_h = qa_ref.shape[1]
    tq = qa_ref.shape[3]
    tk = tq
    sub = lax.broadcasted_iota(jnp.int32, (tk, tq), 0)
    lan = lax.broadcasted_iota(jnp.int32, (tk, tq), 1)
    causal = sub <= lan

    def scores(j, hh):
        k_j = ka_ref[0, hh, pl.ds(pl.multiple_of(j * tk, tk), tk), :]
        return _dot(k_j, qa_ref[0, hh])

    def normalised(acc):
        return acc[0:head_dim] / acc[head_dim:head_dim + 1]

    def block_units(j, j_next, s_first, masked):
        s_next = s_first
        for hh in range(n_h):
            s = jnp.where(causal, s_next, NEG) if masked else s_next
            s_next = scores(j, hh + 1) if hh + 1 < n_h else scores(j_next, 0)
            blk_top = jnp.max(s, axis=0, keepdims=True)
            pv = _dot(va_ref[0, hh, j], jnp.exp2(s).astype(BF16))
            if masked:
                top_ref[hh], acc_ref[hh] = blk_top, pv
            else:
                top_ref[hh] = jnp.maximum(top_ref[hh], blk_top)
                acc_ref[hh] += pv
        return s_next

    s_ref[...] = block_units(i, 0, scores(i, 0), True)

    def block_pair(p, _):
        s_mid = block_units(2 * p, 2 * p + 1, s_ref[...], False)
        s_ref[...] = block_units(2 * p + 1, 2 * p + 2, s_mid, False)
        return 0

    lax.fori_loop(0, i // 2, block_pair, 0)

    @pl.when(i % 2 == 1)
    def _():
        block_units(i - 1, i, s_ref[...], False)

    o_ref[0] = jnp.concatenate([normalised(acc_ref[hh]) for hh in range(n_h)], axis=0).T
    hi, lo = top_ref[0], top_ref[0]
    for hh in range(1, n_h):
        hi, lo = jnp.maximum(hi, top_ref[hh]), jnp.minimum(lo, top_ref[hh])

    @pl.when((jnp.max(hi) > SHIFT_SLACK) | (jnp.min(lo) < -SHIFT_SLACK))
    def _():
        def online_head(hh, _):
            q_t = qa_ref[0, hh]

            def sc(j):
                return _dot(ka_ref[0, hh, pl.ds(pl.multiple_of(j * tk, tk), tk), :], q_t)

            s = jnp.where(causal, sc(i), NEG)
            m = jnp.max(s, axis=0, keepdims=True)
            acc = _dot(va_ref[0, hh, i], jnp.exp2(s - m).astype(BF16))

            def body(j, carry):
                m, acc = carry
                s = sc(j)
                m_new = jnp.maximum(m, jnp.max(s, axis=0, keepdims=True))
                p = jnp.exp2(s - m_new).astype(BF16)
                return m_new, jnp.exp2(m - m_new) * acc + _dot(va_ref[0, hh, j], p)

            m, acc = lax.fori_loop(0, i, body, (m, acc))
            ot_ref[pl.ds(pl.multiple_of(hh * head_dim, head_dim), head_dim), :] = normalised(acc)
            return 0

        lax.fori_loop(0, n_h, online_head, 0)
        o_ref[0] = ot_ref[...].T


def _attn_prompt(qa, ka, va, *, head_dim):
    batch, n_heads, _, seq = qa.shape
    nb, tq = va.shape[2], va.shape[4]
    hps = HEADS_PER_STEP
    resident = dict(pipeline_mode=pl.Buffered(1))
    return pl.pallas_call(
        functools.partial(_attn_prompt_body, head_dim=head_dim),
        out_shape=jax.ShapeDtypeStruct((batch, seq, n_heads * head_dim), F32),
        grid=(batch, n_heads // hps, nb),
        in_specs=[
            pl.BlockSpec((1, hps, AUG, tq), lambda b, g, i: (b, g, 0, i)),
            pl.BlockSpec((1, hps, seq, AUG), lambda b, g, i: (b, g, 0, 0), **resident),
            pl.BlockSpec((1, hps, nb, V_ROWS, tq), lambda b, g, i: (b, g, 0, 0, 0), **resident),
        ],
        out_specs=pl.BlockSpec((1, tq, hps * head_dim), lambda b, g, i: (b, i, g)),
        scratch_shapes=[pltpu.VMEM((hps, V_ROWS, tq), F32), pltpu.VMEM((hps, 1, tq), F32), pltpu.VMEM((tq, tq), F32),
                        pltpu.VMEM((hps * head_dim, tq), F32)],
        compiler_params=pltpu.CompilerParams(dimension_semantics=("arbitrary",) * 3, vmem_limit_bytes=VMEM_LIMIT),
        name="attn_prompt",
    )(qa, ka, va)


def _suffix_body(x_ref, rhs_ref, o_ref):
    hi, mid, lo = _split3(x_ref[...])
    rhs = rhs_ref[...]
    o_ref[...] = _dot(hi.astype(BF16), rhs) + _dot(mid.astype(BF16), rhs) + _dot(lo.astype(BF16), rhs)


def _suffix_pool(lf_pool):
    n_pool, n_heads, page = lf_pool.shape
    rows = n_pool * n_heads
    tr = SUFFIX_ROWS
    assert rows % tr == 0
    r = jnp.arange(page)
    rhs = jnp.concatenate([(r[:, None] > r[None, :]).astype(BF16), jnp.ones((page, page), BF16)], axis=1)
    out = pl.pallas_call(
        _suffix_body,
        out_shape=jax.ShapeDtypeStruct((rows, 2 * page), F32),
        grid=(rows // tr,),
        in_specs=[pl.BlockSpec((tr, page), lambda i: (i, 0)), _const_spec(rhs)],
        out_specs=pl.BlockSpec((tr, 2 * page), lambda i: (i, 0)),
        compiler_params=pltpu.CompilerParams(dimension_semantics=("arbitrary",), vmem_limit_bytes=VMEM_LIMIT),
        name="suffix_pool",
    )(lf_pool.reshape(rows, page), rhs)
    return out.reshape(n_pool, n_heads, 2 * page)


_N_SAMPLE_IN = 9


def _sample_chunk_pair(t, pt_ref, qbd_ref, cq_ref, knew_ref, vnew_ref, cnew_ref, kt_hbm, vt_hbm, sfx_hbm, o_ref,
                       kbuf, vbuf, sbuf, sem, m_ref, l_ref, lsum_ref, top_ref, acc_ref, carry_ref,
                       *, g, n_pages, seq, n_heads, head_dim):
    dec_batch = qbd_ref.shape[0]
    n_new = knew_ref.shape[1]
    per_row = n_pages // g
    n_chunks = dec_batch * per_row

    def first_page(c):
        return (c // per_row) * n_pages + (n_pages - 1) - (c % per_row) * g

    def page_copies(first, slot, p_i):
        page = pt_ref[first - p_i]
        return (pltpu.make_async_copy(kt_hbm.at[page], kbuf.at[slot, p_i], sem.at[0, slot]),
                pltpu.make_async_copy(vt_hbm.at[page], vbuf.at[slot, p_i], sem.at[1, slot]),
                pltpu.make_async_copy(sfx_hbm.at[page], sbuf.at[slot, p_i], sem.at[2, slot]))

    def start_chunk(c, slot):
        first = first_page(c)

        def body(p_i, _):
            for cp in page_copies(first, slot, p_i):
                cp.start()
            return 0
        lax.fori_loop(0, g, body, 0, unroll=COPY_UNROLL)

    def wait_chunk(c, slot):
        first = first_page(c)

        def body(p_i, _):
            for cp in page_copies(first, slot, p_i):
                cp.wait()
            return 0
        lax.fori_loop(0, g, body, 0, unroll=COPY_UNROLL)

    def start_row(b):
        qb, cq = qbd_ref[b], cq_ref[b]
        s = _dot_nt(qb, knew_ref[b]) + cq[:, 0:n_new] - cnew_ref[b]
        t_row = lax.broadcasted_iota(jnp.int32, s.shape, 0) // n_heads
        j = lax.broadcasted_iota(jnp.int32, s.shape, 1)
        s = jnp.where((j <= t_row) & (j < seq), s, NEG)
        m = jnp.max(s, axis=1, keepdims=True)
        p = jnp.exp(s - m)
        m_ref[...] = m
        l_ref[...] = jnp.sum(p, axis=1, keepdims=True)
        acc_ref[...] = _dot(p.astype(BF16), vnew_ref[b])
        lsum_ref[...] = jnp.zeros_like(lsum_ref)
        top_ref[...] = jnp.full_like(top_ref, NEG)
        carry_ref[...] = jnp.zeros_like(carry_ref)

    def page_scores(b, slot):
        qb, cq = qbd_ref[b], cq_ref[b]
        carry = carry_ref[...]
        parts = []
        for p_i in range(g):
            sfx = sbuf[slot, p_i]
            suffix = sfx[:, 0:LANES] + carry
            carry = carry + sfx[:, LANES:2 * LANES]
            bias = jnp.concatenate([suffix] * seq, axis=0) + cq
            parts.append(_dot(qb, kbuf[slot, p_i].astype(BF16)) + bias)
        carry_ref[...] = carry
        return parts

    def one_pass_chunk(b, slot):
        m = m_ref[...]
        top, lsum, acc = top_ref[...], lsum_ref[...], acc_ref[...]
        for p_i, s in enumerate(page_scores(b, slot)):
            top = jnp.maximum(top, s)
            p = jnp.exp(s - m)
            lsum = lsum + p
            acc = acc + _dot_nt(p.astype(BF16), vbuf[slot, p_i].astype(BF16))
        top_ref[...], lsum_ref[...], acc_ref[...] = top, lsum, acc

    def online_chunk(b, slot):
        s = jnp.concatenate(page_scores(b, slot), axis=1)
        m_prev = m_ref[...]
        m_new = jnp.maximum(m_prev, jnp.max(s, axis=1, keepdims=True))
        p = jnp.exp(s - m_new)
        alpha = jnp.exp(m_prev - m_new)
        l_ref[...] = alpha * l_ref[...] + jnp.sum(p, axis=1, keepdims=True)
        pb = p.astype(BF16)
        acc = alpha * acc_ref[...]
        for p_i in range(g):
            acc = acc + _dot_nt(pb[:, p_i * LANES:(p_i + 1) * LANES], vbuf[slot, p_i].astype(BF16))
        acc_ref[...] = acc
        m_ref[...] = m_new

    def write_row(b, l):
        o = acc_ref[...] / l
        r = lax.broadcasted_iota(jnp.int32, o.shape, 0)
        col = lax.broadcasted_iota(jnp.int32, o.shape, 1)
        o = jnp.where(col // head_dim == r % n_heads, o, 0.0)
        o_ref[b] = jnp.sum(o.reshape(seq, n_heads, o.shape[1]), axis=1)

    def finish_row(b, slot):
        write_row(b, l_ref[...] + jnp.sum(lsum_ref[...], axis=1, keepdims=True))
        excess = jnp.max(top_ref[...], axis=1, keepdims=True) - m_ref[...]

        @pl.when(jnp.max(excess) > SAMPLE_SLACK)
        def _():
            start_row(b)

            def redo(r, _):
                start_chunk(b * per_row + r, slot)
                wait_chunk(b * per_row + r, slot)
                online_chunk(b, slot)
                return 0

            lax.fori_loop(0, per_row, redo, 0)
            write_row(b, l_ref[...])

    @pl.when(t == 0)
    def _():
        start_chunk(0, 0)
        start_chunk(1, 1)

    for slot in (0, 1):
        c = 2 * t + slot
        b, r = c // per_row, c % per_row

        @pl.when(r == 0)
        def _():
            start_row(b)

        wait_chunk(c, slot)
        one_pass_chunk(b, slot)

        if slot == 1:
            @pl.when(r == per_row - 1)
            def _():
                finish_row(b, slot)

        @pl.when(c + 2 < n_chunks)
        def _():
            start_chunk(c + 2, slot)


def _sample_specs(pt_flat, qbd, cq, knew, vnew, cnew, kt_pool, vt_pool, sfx_pool, *, g, n_pages, seq, n_heads,
                  head_dim):
    dec_batch, rows, width = qbd.shape
    page = kt_pool.shape[2]
    assert n_pages % (2 * g) == 0 and pt_flat.shape[0] == dec_batch * n_pages
    vmem = pl.BlockSpec(memory_space=pltpu.VMEM)
    hbm = pl.BlockSpec(memory_space=pl.ANY)
    in_specs = [pl.BlockSpec(memory_space=pltpu.SMEM)] + [vmem] * 5 + [hbm] * 3
    scratch = [
        pltpu.VMEM((2, g, width, page), F32), pltpu.VMEM((2, g, width, page), F32),
        pltpu.VMEM((2, g, n_heads, 2 * page), F32), pltpu.SemaphoreType.DMA((3, 2)),
        pltpu.VMEM((rows, 1), F32), pltpu.VMEM((rows, 1), F32), pltpu.VMEM((rows, LANES), F32),
        pltpu.VMEM((rows, LANES), F32), pltpu.VMEM((rows, width), F32), pltpu.VMEM((n_heads, LANES), F32),
    ]
    return in_specs, jax.ShapeDtypeStruct((dec_batch, seq, width), F32), scratch, dec_batch * n_pages // (2 * g)


def kernel(x_prompt, x_sample, cache_k, cache_v, cache_logf, state_conv, page_table, g_ffn1, w1_ffn1, w3_ffn1,
           w2_ffn1, g_mix, w_in, b_f, conv_w, g_attn_out, g_conv_out, w_out, g_ffn2, w1_ffn2, w3_ffn2, w2_ffn2,
           g_final):
    batch, seq, d_model = x_prompt.shape
    dec_batch, dec_seq, _ = x_sample.shape
    depth, n_pool, page, n_heads, head_dim = cache_k.shape
    width = n_heads * head_dim
    assert depth == 1 and seq % ROW_TILE == 0 and page == LANES and 2 * head_dim == LANES
    assert w_in.shape[2] == 6 * width + n_heads

    xp = x_prompt.reshape(batch * seq, d_model)
    xs = x_sample.reshape(dec_batch * dec_seq, d_model)
    row = lambda a: a.reshape(1, -1)
    l = 0

    bw = lambda a: a.astype(BF16)
    wi = w_in[l]
    wf = jnp.tile(wi[:, 3 * width:3 * width + n_heads], (1, 3))
    w_proj = bw(jnp.concatenate([wi[:, :3 * width], wi[:, 3 * width + n_heads:],
                                 jnp.pad(wf, ((0, 0), (0, LANES - 3 * n_heads)))], axis=1))
    bf = jnp.pad(jnp.tile(b_f[l], 3), (0, LANES - 3 * n_heads)).reshape(1, LANES)
    ffn1 = (row(g_ffn1[l]), bw(w1_ffn1[l]), bw(w3_ffn1[l]), bw(w2_ffn1[l]))
    ffn2 = (row(g_ffn2[l]), bw(w1_ffn2[l]), bw(w3_ffn2[l]), bw(w2_ffn2[l]))
    wo = bw(w_out[l])
    merge_w = (row(g_attn_out[l]), wo[:width], wo[width:])
    gmix, cw, gco, gfin = row(g_mix[l]), conv_w[l], row(g_conv_out[l]), row(g_final)

    xs1 = _ffn(xs, *ffn1)
    st = state_conv[l]
    s1 = jnp.repeat(st[:, 1:2], dec_seq, axis=1).reshape(dec_batch * dec_seq, width)
    s2 = jnp.pad(st, ((0, 0), (0, dec_seq - 2), (0, 0))).reshape(dec_batch * dec_seq, width)
    q_s, k_s, v_s, logf_s, c_s, uc_s, cn_s = _proj_sample(
        xs1, gmix, w_proj, bf, cw, gco, s1, s2, seq=dec_seq, n_heads=n_heads, head_dim=head_dim)

    eye = jnp.eye(n_heads, dtype=F32)
    q4 = q_s.reshape(dec_batch, dec_seq, n_heads, 1, head_dim)
    qbd = bw((q4 * eye[None, None, :, :, None]).reshape(dec_batch, dec_seq * n_heads, width))
    c4 = c_s[:, :n_heads].reshape(dec_batch, dec_seq, n_heads)
    cq = jnp.broadcast_to(c4.reshape(dec_batch, dec_seq * n_heads, 1), (dec_batch, dec_seq * n_heads, LANES))
    n_new = 16
    cnew = jnp.broadcast_to(jnp.transpose(c4, (0, 2, 1))[:, None], (dec_batch, dec_seq, n_heads, dec_seq))
    cnew = jnp.pad(cnew.reshape(dec_batch, dec_seq * n_heads, dec_seq), ((0, 0), (0, 0), (0, n_new - dec_seq)))
    pad_new = lambda a: bw(jnp.pad(a.reshape(dec_batch, dec_seq, width), ((0, 0), (0, n_new - dec_seq), (0, 0))))
    kt_pool = jnp.transpose(cache_k[l], (0, 2, 3, 1)).reshape(n_pool, width, page)
    vt_pool = jnp.transpose(cache_v[l], (0, 2, 3, 1)).reshape(n_pool, width, page)
    sfx_pool = _suffix_pool(jnp.transpose(cache_logf[l], (0, 2, 1)))
    n_pages = page_table.shape[1]
    half = dec_batch // 2
    static = dict(g=PAGES_PER_STEP, n_pages=n_pages, seq=dec_seq, n_heads=n_heads, head_dim=head_dim)

    def sample_part(lo):
        rows = slice(lo, lo + half)
        per_row = (page_table[rows].reshape(-1), qbd[rows], cq[rows], pad_new(k_s)[rows], pad_new(v_s)[rows],
                   cnew[rows])
        return per_row + (kt_pool, vt_pool, sfx_pool), static

    xp1, o_s_lo = _ffn(xp, *ffn1, sample=sample_part(0))
    kt_p, vt_p, logft_p, qa, ka, va, cn_p, tail_p = _proj_prompt(
        xp1, gmix, w_proj, bf, cw, gco, batch=batch, n_heads=n_heads, head_dim=head_dim)
    o_p = _attn_prompt(qa, ka, va, head_dim=head_dim).reshape(batch * seq, width)
    y_p, o_s_hi = _ffn(xp1, *ffn2, merge=(o_p, cn_p) + merge_w, g_final=gfin, sample=sample_part(half))

    o_s = jnp.concatenate([o_s_lo, o_s_hi], axis=0).reshape(dec_batch * dec_seq, width)
    y_s = _ffn(xs1, *ffn2, merge=(o_s, cn_s) + merge_w, g_final=gfin)

    hd = (n_heads, head_dim)
    untranspose = lambda a: jnp.transpose(a.reshape(batch, n_heads, head_dim, seq), (0, 3, 1, 2))[None]
    return (
        y_p.reshape(batch, seq, d_model),
        y_s.reshape(dec_batch, dec_seq, d_model),
        untranspose(kt_p),
        untranspose(vt_p),
        jnp.transpose(logft_p, (0, 2, 1))[None],
        tail_p.reshape(1, batch, 2, width),
        k_s.reshape(1, dec_batch, dec_seq, *hd),
        v_s.reshape(1, dec_batch, dec_seq, *hd),
        logf_s.reshape(1, dec_batch, dec_seq, n_heads),
        uc_s.reshape(1, dec_batch, dec_seq, width)[:, :, dec_seq - 2:],
    )
```

```python
import functools
import math

import jax
import jax.numpy as jnp
from jax import lax
from jax.experimental import pallas as pl
from jax.experimental.pallas import tpu as pltpu

F32 = jnp.float32
BF16 = jnp.bfloat16
EPS = 1e-6
NEG = -1e30
LANES = 128
ROW_TILE = 512
SAMPLE_ROW_TILE = 256
PAGES_PER_STEP = 16
HEADS_PER_STEP = 8
SHIFT_SLACK = 64.0
COPY_UNROLL = 4
SAMPLE_SLACK = 40.0
SUFFIX_ROWS = 2048
LOG2E = 1.4426950408889634
VMEM_LIMIT = 56 * 1024 * 1024
AUG = 128
V_ROWS = 80


def _dot(a, b):
    return jnp.dot(a, b, preferred_element_type=F32)


def _dot_nt(a, b):
    return lax.dot_general(a, b, (((1,), (1,)), ((), ())), preferred_element_type=F32)


def _rms(x, g):
    return x * lax.rsqrt(jnp.mean(x * x, axis=-1, keepdims=True) + EPS) * g


def _split3(x):
    hi = x.astype(BF16).astype(F32)
    r1 = x - hi
    mid = r1.astype(BF16).astype(F32)
    lo = (r1 - mid).astype(BF16).astype(F32)
    return hi, mid, lo


def _const_spec(a):
    nd = a.ndim
    return pl.BlockSpec(a.shape, lambda *_: (0,) * nd, pipeline_mode=pl.Buffered(1))


def _ffn_body(*refs, merge, final, sample):
    x_ref, g_ref, w1_ref, w3_ref, w2_ref = refs[:5]
    rest = refs[5:]
    if merge:
        o_ref, cn_ref, gao_ref, woa_ref, woc_ref = rest[:5]
        rest = rest[5:]
    if final:
        gfin_ref = rest[0]
        rest = rest[1:]
    if sample:
        sample_in, rest = rest[:_N_SAMPLE_IN], rest[_N_SAMPLE_IN:]
        out_ref, os_ref = rest[:2]
        _sample_chunk_pair(pl.program_id(0), *sample_in, os_ref, *rest[2:], **sample)
    else:
        out_ref = rest[0]
    x = x_ref[...]
    if merge:
        an = _rms(o_ref[...], gao_ref[...]).astype(BF16)
        x = x + (_dot(an, woa_ref[...]) + _dot(cn_ref[...], woc_ref[...]))
    h = _rms(x, g_ref[...]).astype(BF16)
    a = _dot(h, w1_ref[...])
    b = _dot(h, w3_ref[...])
    act = (a * jax.nn.sigmoid(a) * b).astype(BF16)
    y = x + 0.5 * _dot(act, w2_ref[...])
    if final:
        y = _rms(y, gfin_ref[...])
    out_ref[...] = y


def _ffn(x, g, w1, w3, w2, *, merge=None, g_final=None, sample=None):
    n, d = x.shape
    tm = min(ROW_TILE if sample is None else SAMPLE_ROW_TILE, n)
    row = lambda i: (i, 0)
    args = [x, g, w1, w3, w2]
    specs = [pl.BlockSpec((tm, d), row)] + [_const_spec(a) for a in args[1:]]
    if merge is not None:
        o, cn, gao, woa, woc = merge
        args += [o, cn, gao, woa, woc]
        specs += [pl.BlockSpec((tm, o.shape[1]), row), pl.BlockSpec((tm, cn.shape[1]), row),
                  _const_spec(gao), _const_spec(woa), _const_spec(woc)]
    if g_final is not None:
        args.append(g_final)
        specs.append(_const_spec(g_final))
    out_shape = jax.ShapeDtypeStruct((n, d), F32)
    out_specs = pl.BlockSpec((tm, d), row)
    scratch, static = [], None
    if sample is not None:
        operands, static = sample
        s_specs, s_out_shape, scratch, n_steps = _sample_specs(*operands, **static)
        assert n // tm == n_steps and len(operands) == _N_SAMPLE_IN
        args += list(operands)
        specs += s_specs
        out_shape = (out_shape, s_out_shape)
        out_specs = (out_specs, pl.BlockSpec(memory_space=pltpu.VMEM))
    return pl.pallas_call(
        functools.partial(_ffn_body, merge=merge is not None, final=g_final is not None, sample=static),
        out_shape=out_shape,
        grid=(n // tm,),
        in_specs=specs,
        out_specs=out_specs,
        scratch_shapes=scratch,
        compiler_params=pltpu.CompilerParams(dimension_semantics=("arbitrary",), vmem_limit_bytes=VMEM_LIMIT),
        name=("ffn_merge" if merge is not None else "ffn") + ("_sample" if sample is not None else ""),
    )(*args)


def _proj_common(x_ref, g_ref, w_ref, bf_ref, tri_ref, width, scale):
    a = width
    h = _rms(x_ref[...], g_ref[...]).astype(BF16)
    p = _dot(h, w_ref[...])
    q = p[:, 0:a] * scale
    k = p[:, a:2 * a]
    v = p[:, 2 * a:3 * a]
    bg = p[:, 3 * a:4 * a]
    uc = p[:, 4 * a:5 * a] * p[:, 5 * a:6 * a]
    z = p[:, 6 * a:6 * a + LANES] + bf_ref[...]
    lane = lax.broadcasted_iota(jnp.int32, z.shape, 1)
    logf = jnp.minimum(z, 0.0) - jnp.log1p(jnp.exp(-jnp.abs(z)))
    l3 = jnp.where(lane < 24, logf, 0.0)
    hi, mid, lo = _split3(l3)
    packed = jnp.where(lane < 8, hi, jnp.where(lane < 16, mid, lo)).astype(BF16)
    r = _dot(tri_ref[...], packed)
    c = r + pltpu.roll(r, LANES - 8, axis=1) + pltpu.roll(r, LANES - 16, axis=1)
    c = jnp.where(lane < 8, c, 0.0)
    return q, k, v, bg, uc, logf, c, lane


def _conv_branch(uc, prev1, prev2, bg, cw_ref, gco_ref):
    y = cw_ref[0:1, :] * prev2 + cw_ref[1:2, :] * prev1 + cw_ref[2:3, :] * uc
    return _rms(bg * y, gco_ref[...]).astype(BF16)


def _proj_prompt_body(x_ref, g_ref, w_ref, bf_ref, tri_ref, cw_ref, gco_ref, psel_ref, asel_ref, hsel_ref,
                      kt_ref, vt_ref, logf_ref, qa_ref, ka_ref, va_ref, cn_ref, tail_ref,
                      carry_c, carry_u, *, n_heads, head_dim, scale):
    tm = x_ref.shape[0]
    width = n_heads * head_dim

    @pl.when(pl.program_id(1) == 0)
    def _():
        carry_c[...] = jnp.zeros_like(carry_c)
        carry_u[...] = jnp.zeros_like(carry_u)

    q, k, v, bg, uc, logf, c, lane = _proj_common(x_ref, g_ref, w_ref, bf_ref, tri_ref, width, scale * LOG2E)
    q_t = q.T
    v_t = v.T
    kt_ref[0] = k.T
    vt_ref[0] = v_t
    logf_ref[0] = logf.T[0:n_heads]

    c = c + carry_c[...]
    carry_c[...] = c[tm - 1:tm, :]
    c2 = c * LOG2E

    def packed_parts(x):
        rep = jnp.where(lane < 8, x, jnp.where(lane < 16, pltpu.roll(x, 8, axis=1),
                                               jnp.where(lane < 24, pltpu.roll(x, 16, axis=1), 0.0)))
        hi, mid, lo = _split3(rep)
        return jnp.where(lane < 8, hi, jnp.where(lane < 16, mid, jnp.where(lane < 24, lo,
                         jnp.where(lane == 24, 1.0, 0.0)))).astype(BF16)

    shift = _dot((q * k).astype(BF16), hsel_ref[...])
    k_extra = _dot(packed_parts(c2), psel_ref[...])
    q_extra = _dot_nt(asel_ref[...], packed_parts(c2 - shift))

    lane_a = lax.broadcasted_iota(jnp.int32, (tm, AUG), 1)
    row8 = lax.broadcasted_iota(jnp.int32, (8, tm), 0)
    ones_row = jnp.where(row8 == 0, 1.0, 0.0)
    pad_q = jnp.zeros((AUG - head_dim - 8, tm), F32)
    pad_v = jnp.zeros((V_ROWS - head_dim - 8, tm), F32)
    for h in range(n_heads):
        blk = k[:, (h // 2) * AUG:(h // 2 + 1) * AUG]
        if h % 2:
            blk = pltpu.roll(blk, head_dim, axis=1)
        ka_ref[0, h] = jnp.where(lane_a < head_dim, blk, k_extra[:, h * AUG:(h + 1) * AUG]).astype(BF16)
        qa_ref[0, h] = jnp.concatenate(
            [q_t[h * head_dim:(h + 1) * head_dim], q_extra[h * 8:(h + 1) * 8], pad_q], axis=0).astype(BF16)
        va_ref[0, h, 0] = jnp.concatenate(
            [v_t[h * head_dim:(h + 1) * head_dim], ones_row, pad_v], axis=0).astype(BF16)

    rows = lax.broadcasted_iota(jnp.int32, uc.shape, 0)
    last1 = carry_u[7:8, :]
    last2 = carry_u[6:7, :]
    prev1 = jnp.where(rows == 0, last1, pltpu.roll(uc, 1, axis=0))
    prev2 = jnp.where(rows == 0, last2, jnp.where(rows == 1, last1, pltpu.roll(uc, 2, axis=0)))
    cn_ref[...] = _conv_branch(uc, prev1, prev2, bg, cw_ref, gco_ref)
    tail_ref[0] = uc[tm - 2:tm, :]
    carry_u[...] = uc[tm - 8:tm, :]


def _proj_sample_body(x_ref, g_ref, w_ref, bf_ref, tri_ref, cw_ref, gco_ref, s1_ref, s2_ref,
                      q_ref, k_ref, v_ref, logf_ref, c_ref, uc_ref, cn_ref, *, n_heads, head_dim, scale, seq):
    q, k, v, bg, uc, logf, c, _ = _proj_common(x_ref, g_ref, w_ref, bf_ref, tri_ref, n_heads * head_dim, scale)
    q_ref[...] = q
    k_ref[...] = k
    v_ref[...] = v
    logf_ref[...] = logf[:, 0:n_heads]
    c_ref[...] = c
    uc_ref[...] = uc
    t = lax.broadcasted_iota(jnp.int32, uc.shape, 0) % seq
    prev1 = jnp.where(t == 0, s1_ref[...], pltpu.roll(uc, 1, axis=0))
    prev2 = jnp.where(t < 2, s2_ref[...], pltpu.roll(uc, 2, axis=0))
    cn_ref[...] = _conv_branch(uc, prev1, prev2, bg, cw_ref, gco_ref)


def _selectors(n_heads):
    psel = jnp.zeros((LANES, n_heads * AUG), F32)
    asel = jnp.zeros((n_heads * 8, LANES), F32)
    for h in range(n_heads):
        base = h * AUG + 64
        psel = psel.at[24, base:base + 3].set(1.0)
        for part in range(3):
            psel = psel.at[part * 8 + h, base + 3 + part].set(-1.0)
            asel = asel.at[h * 8 + part, part * 8 + h].set(1.0)
        asel = asel.at[h * 8 + 3:h * 8 + 6, 24].set(1.0)
    return psel.astype(BF16), asel.astype(BF16)


def _proj_prompt(x, g, w, bf, cw, gco, *, batch, n_heads, head_dim):
    n, d = x.shape
    seq = n // batch
    tm = ROW_TILE
    nb = seq // tm
    width = n_heads * head_dim
    tri = jnp.tril(jnp.ones((tm, tm), F32)).astype(BF16)
    psel, asel = _selectors(n_heads)
    hsel = (jnp.arange(width)[:, None] // head_dim == jnp.arange(LANES)[None, :]).astype(BF16)
    row = lambda b, i: (b * nb + i, 0)
    consts = [g, w, bf, tri, cw, gco, psel, asel, hsel]
    out_shape = (
        jax.ShapeDtypeStruct((batch, width, seq), F32),
        jax.ShapeDtypeStruct((batch, width, seq), F32),
        jax.ShapeDtypeStruct((batch, n_heads, seq), F32),
        jax.ShapeDtypeStruct((batch, n_heads, AUG, seq), BF16),
        jax.ShapeDtypeStruct((batch, n_heads, seq, AUG), BF16),
        jax.ShapeDtypeStruct((batch, n_heads, nb, V_ROWS, tm), BF16),
        jax.ShapeDtypeStruct((n, width), BF16),
        jax.ShapeDtypeStruct((batch, 2, width), F32),
    )
    out_specs = (
        pl.BlockSpec((1, width, tm), lambda b, i: (b, 0, i)),
        pl.BlockSpec((1, width, tm), lambda b, i: (b, 0, i)),
        pl.BlockSpec((1, n_heads, tm), lambda b, i: (b, 0, i)),
        pl.BlockSpec((1, n_heads, AUG, tm), lambda b, i: (b, 0, 0, i)),
        pl.BlockSpec((1, n_heads, tm, AUG), lambda b, i: (b, 0, i, 0)),
        pl.BlockSpec((1, n_heads, 1, V_ROWS, tm), lambda b, i: (b, 0, i, 0, 0)),
        pl.BlockSpec((tm, width), row),
        pl.BlockSpec((1, 2, width), lambda b, i: (b, 0, 0)),
    )
    return pl.pallas_call(
        functools.partial(_proj_prompt_body, n_heads=n_heads, head_dim=head_dim, scale=1.0 / math.sqrt(head_dim)),
        out_shape=out_shape,
        grid=(batch, nb),
        in_specs=[pl.BlockSpec((tm, d), row)] + [_const_spec(a) for a in consts],
        out_specs=out_specs,
        scratch_shapes=[pltpu.VMEM((1, LANES), F32), pltpu.VMEM((8, width), F32)],
        compiler_params=pltpu.CompilerParams(dimension_semantics=("arbitrary", "arbitrary"),
                                             vmem_limit_bytes=VMEM_LIMIT),
        name="proj_prompt",
    )(x, *consts)


def _proj_sample(x, g, w, bf, cw, gco, s1, s2, *, seq, n_heads, head_dim):
    n, d = x.shape
    width = n_heads * head_dim
    r = jnp.arange(n)
    tri = ((r[None, :] <= r[:, None]) & (r[None, :] // seq == r[:, None] // seq)).astype(BF16)
    args = [x, g, w, bf, tri, cw, gco, s1, s2]
    shapes = [(n, width)] * 3 + [(n, n_heads), (n, LANES), (n, width)]
    out_shape = tuple(jax.ShapeDtypeStruct(s, F32) for s in shapes) + (jax.ShapeDtypeStruct((n, width), BF16),)
    return pl.pallas_call(
        functools.partial(_proj_sample_body, n_heads=n_heads, head_dim=head_dim,
                          scale=1.0 / math.sqrt(head_dim), seq=seq),
        out_shape=out_shape,
        compiler_params=pltpu.CompilerParams(vmem_limit_bytes=VMEM_LIMIT),
        name="proj_sample",
    )(*args)


def _attn_prompt_body(qa_ref, ka_ref, va_ref, o_ref, acc_ref, top_ref, s_ref, ot_ref, *, head_dim):
    i = pl.program_id(2)
    n_h = qa_ref.shape[1]
    tq = qa_ref.shape[3]
    tk = tq
    sub = lax.broadcasted_iota(jnp.int32, (tk, tq), 0)
    lan = lax.broadcasted_iota(jnp.int32, (tk, tq), 1)
    causal = sub <= lan

    def scores(j, hh):
        k_j = ka_ref[0, hh, pl.ds(pl.multiple_of(j * tk, tk), tk), :]
        return _dot(k_j, qa_ref[0, hh])

    def normalised(acc):
        return acc[0:head_dim] / acc[head_dim:head_dim + 1]

    def block_units(j, j_next, s_first, masked):
        s_next = s_first
        for hh in range(n_h):
            s = jnp.where(causal, s_next, NEG) if masked else s_next
            s_next = scores(j, hh + 1) if hh + 1 < n_h else scores(j_next, 0)
            blk_top = jnp.max(s, axis=0, keepdims=True)
            pv = _dot(va_ref[0, hh, j], jnp.exp2(s).astype(BF16))
            if masked:
                top_ref[hh], acc_ref[hh] = blk_top, pv
            else:
                top_ref[hh] = jnp.maximum(top_ref[hh], blk_top)
                acc_ref[hh] += pv
        return s_next

    s_ref[...] = block_units(i, 0, scores(i, 0), True)

    def block_pair(p, _):
        s_mid = block_units(2 * p, 2 * p + 1, s_ref[...], False)
        s_ref[...] = block_units(2 * p + 1, 2 * p + 2, s_mid, False)
        return 0

    lax.fori_loop(0, i // 2, block_pair, 0)

    @pl.when(i % 2 == 1)
    def _():
        block_units(i - 1, i, s_ref[...], False)

    o_ref[0] = jnp.concatenate([normalised(acc_ref[hh]) for hh in range(n_h)], axis=0).T
    hi, lo = top_ref[0], top_ref[0]
    for hh in range(1, n_h):
        hi, lo = jnp.maximum(hi, top_ref[hh]), jnp.minimum(lo, top_ref[hh])

    @pl.when((jnp.max(hi) > SHIFT_SLACK) | (jnp.min(lo) < -SHIFT_SLACK))
    def _():
        def online_head(hh, _):
            q_t = qa_ref[0, hh]

            def sc(j):
                return _dot(ka_ref[0, hh, pl.ds(pl.multiple_of(j * tk, tk), tk), :], q_t)

            s = jnp.where(causal, sc(i), NEG)
            m = jnp.max(s, axis=0, keepdims=True)
            acc = _dot(va_ref[0, hh, i], jnp.exp2(s - m).astype(BF16))

            def body(j, carry):
                m, acc = carry
                s = sc(j)
                m_new = jnp.maximum(m, jnp.max(s, axis=0, keepdims=True))
                p = jnp.exp2(s - m_new).astype(BF16)
                return m_new, jnp.exp2(m - m_new) * acc + _dot(va_ref[0, hh, j], p)

            m, acc = lax.fori_loop(0, i, body, (m, acc))
            ot_ref[pl.ds(pl.multiple_of(hh * head_dim, head_dim), head_dim), :] = normalised(acc)
            return 0

        lax.fori_loop(0, n_h, online_head, 0)
        o_ref[0] = ot_ref[...].T


def _attn_prompt(qa, ka, va, *, head_dim):
    batch, n_heads, _, seq = qa.shape
    nb, tq = va.shape[2], va.shape[4]
    hps = HEADS_PER_STEP
    resident = dict(pipeline_mode=pl.Buffered(1))
    return pl.pallas_call(
        functools.partial(_attn_prompt_body, head_dim=head_dim),
        out_shape=jax.ShapeDtypeStruct((batch, seq, n_heads * head_dim), F32),
        grid=(batch, n_heads // hps, nb),
        in_specs=[
            pl.BlockSpec((1, hps, AUG, tq), lambda b, g, i: (b, g, 0, i)),
            pl.BlockSpec((1, hps, seq, AUG), lambda b, g, i: (b, g, 0, 0), **resident),
            pl.BlockSpec((1, hps, nb, V_ROWS, tq), lambda b, g, i: (b, g, 0, 0, 0), **resident),
        ],
        out_specs=pl.BlockSpec((1, tq, hps * head_dim), lambda b, g, i: (b, i, g)),
        scratch_shapes=[pltpu.VMEM((hps, V_ROWS, tq), F32), pltpu.VMEM((hps, 1, tq), F32), pltpu.VMEM((tq, tq), F32),
                        pltpu.VMEM((hps * head_dim, tq), F32)],
        compiler_params=pltpu.CompilerParams(dimension_semantics=("arbitrary",) * 3, vmem_limit_bytes=VMEM_LIMIT),
        name="attn_prompt",
    )(qa, ka, va)


def _suffix_body(x_ref, rhs_ref, o_ref):
    hi, mid, lo = _split3(x_ref[...])
    rhs = rhs_ref[...]
    o_ref[...] = _dot(hi.astype(BF16), rhs) + _dot(mid.astype(BF16), rhs) + _dot(lo.astype(BF16), rhs)


def _suffix_pool(lf_pool):
    n_pool, n_heads, page = lf_pool.shape
    rows = n_pool * n_heads
    tr = SUFFIX_ROWS
    assert rows % tr == 0
    r = jnp.arange(page)
    rhs = jnp.concatenate([(r[:, None] > r[None, :]).astype(BF16), jnp.ones((page, page), BF16)], axis=1)
    out = pl.pallas_call(
        _suffix_body,
        out_shape=jax.ShapeDtypeStruct((rows, 2 * page), F32),
        grid=(rows // tr,),
        in_specs=[pl.BlockSpec((tr, page), lambda i: (i, 0)), _const_spec(rhs)],
        out_specs=pl.BlockSpec((tr, 2 * page), lambda i: (i, 0)),
        compiler_params=pltpu.CompilerParams(dimension_semantics=("arbitrary",), vmem_limit_bytes=VMEM_LIMIT),
        name="suffix_pool",
    )(lf_pool.reshape(rows, page), rhs)
    return out.reshape(n_pool, n_heads, 2 * page)


_N_SAMPLE_IN = 9


def _sample_chunk_pair(t, pt_ref, qbd_ref, cq_ref, knew_ref, vnew_ref, cnew_ref, kt_hbm, vt_hbm, sfx_hbm, o_ref,
                       kbuf, vbuf, sbuf, sem, m_ref, l_ref, lsum_ref, top_ref, acc_ref, carry_ref,
                       *, g, n_pages, seq, n_heads, head_dim):
    dec_batch = qbd_ref.shape[0]
    n_new = knew_ref.shape[1]
    per_row = n_pages // g
    n_chunks = dec_batch * per_row

    def first_page(c):
        return (c // per_row) * n_pages + (n_pages - 1) - (c % per_row) * g

    def page_copies(first, slot, p_i):
        page = pt_ref[first - p_i]
        return (pltpu.make_async_copy(kt_hbm.at[page], kbuf.at[slot, p_i], sem.at[0, slot]),
                pltpu.make_async_copy(vt_hbm.at[page], vbuf.at[slot, p_i], sem.at[1, slot]),
                pltpu.make_async_copy(sfx_hbm.at[page], sbuf.at[slot, p_i], sem.at[2, slot]))

    def start_chunk(c, slot):
        first = first_page(c)

        def body(p_i, _):
            for cp in page_copies(first, slot, p_i):
                cp.start()
            return 0
        lax.fori_loop(0, g, body, 0, unroll=COPY_UNROLL)

    def wait_chunk(c, slot):
        first = first_page(c)

        def body(p_i, _):
            for cp in page_copies(first, slot, p_i):
                cp.wait()
            return 0
        lax.fori_loop(0, g, body, 0, unroll=COPY_UNROLL)

    def start_row(b):
        qb, cq = qbd_ref[b], cq_ref[b]
        s = _dot_nt(qb, knew_ref[b]) + cq[:, 0:n_new] - cnew_ref[b]
        t_row = lax.broadcasted_iota(jnp.int32, s.shape, 0) // n_heads
        j = lax.broadcasted_iota(jnp.int32, s.shape, 1)
        s = jnp.where((j <= t_row) & (j < seq), s, NEG)
        m = jnp.max(s, axis=1, keepdims=True)
        p = jnp.exp(s - m)
        m_ref[...] = m
        l_ref[...] = jnp.sum(p, axis=1, keepdims=True)
        acc_ref[...] = _dot(p.astype(BF16), vnew_ref[b])
        lsum_ref[...] = jnp.zeros_like(lsum_ref)
        top_ref[...] = jnp.full_like(top_ref, NEG)
        carry_ref[...] = jnp.zeros_like(carry_ref)

    def page_scores(b, slot):
        qb, cq = qbd_ref[b], cq_ref[b]
        carry = carry_ref[...]
        parts = []
        for p_i in range(g):
            sfx = sbuf[slot, p_i]
            suffix = sfx[:, 0:LANES] + carry
            carry = carry + sfx[:, LANES:2 * LANES]
            bias = jnp.concatenate([suffix] * seq, axis=0) + cq
            parts.append(_dot(qb, kbuf[slot, p_i].astype(BF16)) + bias)
        carry_ref[...] = carry
        return parts

    def page_values(slot, p_i):
        return vbuf[slot, p_i].T.astype(BF16)

    def one_pass_chunk(b, slot):
        m = m_ref[...]
        top, lsum, acc = top_ref[...], lsum_ref[...], acc_ref[...]
        for p_i, s in enumerate(page_scores(b, slot)):
            top = jnp.maximum(top, s)
            p = jnp.exp(s - m)
            lsum = lsum + p
            acc = acc + _dot(p.astype(BF16), page_values(slot, p_i))
        top_ref[...], lsum_ref[...], acc_ref[...] = top, lsum, acc

    def online_chunk(b, slot):
        s = jnp.concatenate(page_scores(b, slot), axis=1)
        m_prev = m_ref[...]
        m_new = jnp.maximum(m_prev, jnp.max(s, axis=1, keepdims=True))
        p = jnp.exp(s - m_new)
        alpha = jnp.exp(m_prev - m_new)
        l_ref[...] = alpha * l_ref[...] + jnp.sum(p, axis=1, keepdims=True)
        pb = p.astype(BF16)
        acc = alpha * acc_ref[...]
        for p_i in range(g):
            acc = acc + _dot(pb[:, p_i * LANES:(p_i + 1) * LANES], page_values(slot, p_i))
        acc_ref[...] = acc
        m_ref[...] = m_new

    def write_row(b, l):
        o = acc_ref[...] / l
        r = lax.broadcasted_iota(jnp.int32, o.shape, 0)
        col = lax.broadcasted_iota(jnp.int32, o.shape, 1)
        o = jnp.where(col // head_dim == r % n_heads, o, 0.0)
        o_ref[b] = jnp.sum(o.reshape(seq, n_heads, o.shape[1]), axis=1)

    def finish_row(b, slot):
        write_row(b, l_ref[...] + jnp.sum(lsum_ref[...], axis=1, keepdims=True))
        excess = jnp.max(top_ref[...], axis=1, keepdims=True) - m_ref[...]

        @pl.when(jnp.max(excess) > SAMPLE_SLACK)
        def _():
            start_row(b)

            def redo(r, _):
                start_chunk(b * per_row + r, slot)
                wait_chunk(b * per_row + r, slot)
                online_chunk(b, slot)
                return 0

            lax.fori_loop(0, per_row, redo, 0)
            write_row(b, l_ref[...])

    @pl.when(t == 0)
    def _():
        start_chunk(0, 0)
        start_chunk(1, 1)

    for slot in (0, 1):
        c = 2 * t + slot
        b, r = c // per_row, c % per_row

        @pl.when(r == 0)
        def _():
            start_row(b)

        wait_chunk(c, slot)
        one_pass_chunk(b, slot)

        if slot == 1:
            @pl.when(r == per_row - 1)
            def _():
                finish_row(b, slot)

        @pl.when(c + 2 < n_chunks)
        def _():
            start_chunk(c + 2, slot)


def _sample_specs(pt_flat, qbd, cq, knew, vnew, cnew, kt_pool, vt_pool, sfx_pool, *, g, n_pages, seq, n_heads,
                  head_dim):
    dec_batch, rows, width = qbd.shape
    page = kt_pool.shape[2]
    assert n_pages % (2 * g) == 0 and pt_flat.shape[0] == dec_batch * n_pages
    vmem = pl.BlockSpec(memory_space=pltpu.VMEM)
    hbm = pl.BlockSpec(memory_space=pl.ANY)
    in_specs = [pl.BlockSpec(memory_space=pltpu.SMEM)] + [vmem] * 5 + [hbm] * 3
    scratch = [
        pltpu.VMEM((2, g, width, page), F32), pltpu.VMEM((2, g, width, page), F32),
        pltpu.VMEM((2, g, n_heads, 2 * page), F32), pltpu.SemaphoreType.DMA((3, 2)),
        pltpu.VMEM((rows, 1), F32), pltpu.VMEM((rows, 1), F32), pltpu.VMEM((rows, LANES), F32),
        pltpu.VMEM((rows, LANES), F32), pltpu.VMEM((rows, width), F32), pltpu.VMEM((n_heads, LANES), F32),
    ]
    return in_specs, jax.ShapeDtypeStruct((dec_batch, seq, width), F32), scratch, dec_batch * n_pages // (2 * g)


def kernel(x_prompt, x_sample, cache_k, cache_v, cache_logf, state_conv, page_table, g_ffn1, w1_ffn1, w3_ffn1,
           w2_ffn1, g_mix, w_in, b_f, conv_w, g_attn_out, g_conv_out, w_out, g_ffn2, w1_ffn2, w3_ffn2, w2_ffn2,
           g_final):
    batch, seq, d_model = x_prompt.shape
    dec_batch, dec_seq, _ = x_sample.shape
    depth, n_pool, page, n_heads, head_dim = cache_k.shape
    width = n_heads * head_dim
    assert depth == 1 and seq % ROW_TILE == 0 and page == LANES and 2 * head_dim == LANES
    assert w_in.shape[2] == 6 * width + n_heads

    xp = x_prompt.reshape(batch * seq, d_model)
    xs = x_sample.reshape(dec_batch * dec_seq, d_model)
    row = lambda a: a.reshape(1, -1)
    l = 0

    bw = lambda a: a.astype(BF16)
    wi = w_in[l]
    wf = jnp.tile(wi[:, 3 * width:3 * width + n_heads], (1, 3))
    w_proj = bw(jnp.concatenate([wi[:, :3 * width], wi[:, 3 * width + n_heads:],
                                 jnp.pad(wf, ((0, 0), (0, LANES - 3 * n_heads)))], axis=1))
    bf = jnp.pad(jnp.tile(b_f[l], 3), (0, LANES - 3 * n_heads)).reshape(1, LANES)
    ffn1 = (row(g_ffn1[l]), bw(w1_ffn1[l]), bw(w3_ffn1[l]), bw(w2_ffn1[l]))
    ffn2 = (row(g_ffn2[l]), bw(w1_ffn2[l]), bw(w3_ffn2[l]), bw(w2_ffn2[l]))
    wo = bw(w_out[l])
    merge_w = (row(g_attn_out[l]), wo[:width], wo[width:])
    gmix, cw, gco, gfin = row(g_mix[l]), conv_w[l], row(g_conv_out[l]), row(g_final)

    xs1 = _ffn(xs, *ffn1)
    st = state_conv[l]
    s1 = jnp.repeat(st[:, 1:2], dec_seq, axis=1).reshape(dec_batch * dec_seq, width)
    s2 = jnp.pad(st, ((0, 0), (0, dec_seq - 2), (0, 0))).reshape(dec_batch * dec_seq, width)
    q_s, k_s, v_s, logf_s, c_s, uc_s, cn_s = _proj_sample(
        xs1, gmix, w_proj, bf, cw, gco, s1, s2, seq=dec_seq, n_heads=n_heads, head_dim=head_dim)

    eye = jnp.eye(n_heads, dtype=F32)
    q4 = q_s.reshape(dec_batch, dec_seq, n_heads, 1, head_dim)
    qbd = bw((q4 * eye[None, None, :, :, None]).reshape(dec_batch, dec_seq * n_heads, width))
    c4 = c_s[:, :n_heads].reshape(dec_batch, dec_seq, n_heads)
    cq = jnp.broadcast_to(c4.reshape(dec_batch, dec_seq * n_heads, 1), (dec_batch, dec_seq * n_heads, LANES))
    n_new = 16
    cnew = jnp.broadcast_to(jnp.transpose(c4, (0, 2, 1))[:, None], (dec_batch, dec_seq, n_heads, dec_seq))
    cnew = jnp.pad(cnew.reshape(dec_batch, dec_seq * n_heads, dec_seq), ((0, 0), (0, 0), (0, n_new - dec_seq)))
    pad_new = lambda a: bw(jnp.pad(a.reshape(dec_batch, dec_seq, width), ((0, 0), (0, n_new - dec_seq), (0, 0))))
    kt_pool = jnp.transpose(cache_k[l], (0, 2, 3, 1)).reshape(n_pool, width, page)
    vt_pool = jnp.transpose(cache_v[l], (0, 2, 3, 1)).reshape(n_pool, width, page)
    sfx_pool = _suffix_pool(jnp.transpose(cache_logf[l], (0, 2, 1)))
    n_pages = page_table.shape[1]
    half = dec_batch // 2
    static = dict(g=PAGES_PER_STEP, n_pages=n_pages, seq=dec_seq, n_heads=n_heads, head_dim=head_dim)

    def sample_part(lo):
        rows = slice(lo, lo + half)
        per_row = (page_table[rows].reshape(-1), qbd[rows], cq[rows], pad_new(k_s)[rows], pad_new(v_s)[rows],
                   cnew[rows])
        return per_row + (kt_pool, vt_pool, sfx_pool), static

    xp1, o_s_lo = _ffn(xp, *ffn1, sample=sample_part(0))
    kt_p, vt_p, logft_p, qa, ka, va, cn_p, tail_p = _proj_prompt(
        xp1, gmix, w_proj, bf, cw, gco, batch=batch, n_heads=n_heads, head_dim=head_dim)
    o_p = _attn_prompt(qa, ka, va, head_dim=head_dim).reshape(batch * seq, width)
    y_p, o_s_hi = _ffn(xp1, *ffn2, merge=(o_p, cn_p) + merge_w, g_final=gfin, sample=sample_part(half))

    o_s = jnp.concatenate([o_s_lo, o_s_hi], axis=0).reshape(dec_batch * dec_seq, width)
    y_s = _ffn(xs1, *ffn2, merge=(o_s, cn_s) + merge_w, g_final=gfin)

    hd = (n_heads, head_dim)
    untranspose = lambda a: jnp.transpose(a.reshape(batch, n_heads, head_dim, seq), (0, 3, 1, 2))[None]
    return (
        y_p.reshape(batch, seq, d_model),
        y_s.reshape(dec_batch, dec_seq, d_model),
        untranspose(kt_p),
        untranspose(vt_p),
        jnp.transpose(logft_p, (0, 2, 1))[None],
        tail_p.reshape(1, batch, 2, width),
        k_s.reshape(1, dec_batch, dec_seq, *hd),
        v_s.reshape(1, dec_batch, dec_seq, *hd),
        logf_s.reshape(1, dec_batch, dec_seq, n_heads),
        uc_s.reshape(1, dec_batch, dec_seq, width)[:, :, dec_seq - 2:],
    )
```

```python
import functools
import math

import jax
import jax.numpy as jnp
from jax import lax
from jax.experimental import pallas as pl
from jax.experimental.pallas import tpu as pltpu

F32 = jnp.float32
BF16 = jnp.bfloat16
EPS = 1e-6
NEG = -1e30
LANES = 128
ROW_TILE = 512
SAMPLE_ROW_TILE = 256
PAGES_PER_STEP = 16
HEADS_PER_STEP = 8
SHIFT_SLACK = 64.0
COPY_UNROLL = 4
SAMPLE_SLACK = 40.0
SUFFIX_ROWS = 2048
LOG2E = 1.4426950408889634
VMEM_LIMIT = 56 * 1024 * 1024
AUG = 128
V_ROWS = 80


def _dot(a, b):
    return jnp.dot(a, b, preferred_element_type=F32)


def _dot_nt(a, b):
    return lax.dot_general(a, b, (((1,), (1,)), ((), ())), preferred_element_type=F32)


def _rms(x, g):
    return x * lax.rsqrt(jnp.mean(x * x, axis=-1, keepdims=True) + EPS) * g


def _split3(x):
    hi = x.astype(BF16).astype(F32)
    r1 = x - hi
    mid = r1.astype(BF16).astype(F32)
    lo = (r1 - mid).astype(BF16).astype(F32)
    return hi, mid, lo


def _const_spec(a):
    nd = a.ndim
    return pl.BlockSpec(a.shape, lambda *_: (0,) * nd, pipeline_mode=pl.Buffered(1))


def _ffn_body(*refs, merge, final, sample):
    x_ref, g_ref, w1_ref, w3_ref, w2_ref = refs[:5]
    rest = refs[5:]
    if merge:
        o_ref, cn_ref, gao_ref, woa_ref, woc_ref = rest[:5]
        rest = rest[5:]
    if final:
        gfin_ref = rest[0]
        rest = rest[1:]
    if sample:
        sample_in, rest = rest[:_N_SAMPLE_IN], rest[_N_SAMPLE_IN:]
        out_ref, os_ref = rest[:2]
        _sample_chunk_pair(pl.program_id(0), *sample_in, os_ref, *rest[2:], **sample)
    else:
        out_ref = rest[0]
    x = x_ref[...]
    if merge:
        an = _rms(o_ref[...], gao_ref[...]).astype(BF16)
        x = x + (_dot(an, woa_ref[...]) + _dot(cn_ref[...], woc_ref[...]))
    h = _rms(x, g_ref[...]).astype(BF16)
    a = _dot(h, w1_ref[...])
    b = _dot(h, w3_ref[...])
    act = (a * jax.nn.sigmoid(a) * b).astype(BF16)
    y = x + 0.5 * _dot(act, w2_ref[...])
    if final:
        y = _rms(y, gfin_ref[...])
    out_ref[...] = y


def _ffn(x, g, w1, w3, w2, *, merge=None, g_final=None, sample=None):
    n, d = x.shape
    tm = min(ROW_TILE if sample is None else SAMPLE_ROW_TILE, n)
    row = lambda i: (i, 0)
    args = [x, g, w1, w3, w2]
    specs = [pl.BlockSpec((tm, d), row)] + [_const_spec(a) for a in args[1:]]
    if merge is not None:
        o, cn, gao, woa, woc = merge
        args += [o, cn, gao, woa, woc]
        specs += [pl.BlockSpec((tm, o.shape[1]), row), pl.BlockSpec((tm, cn.shape[1]), row),
                  _const_spec(gao), _const_spec(woa), _const_spec(woc)]
    if g_final is not None:
        args.append(g_final)
        specs.append(_const_spec(g_final))
    out_shape = jax.ShapeDtypeStruct((n, d), F32)
    out_specs = pl.BlockSpec((tm, d), row)
    scratch, static = [], None
    if sample is not None:
        operands, static = sample
        s_specs, s_out_shape, scratch, n_steps = _sample_specs(*operands, **static)
        assert n // tm == n_steps and len(operands) == _N_SAMPLE_IN
        args += list(operands)
        specs += s_specs
        out_shape = (out_shape, s_out_shape)
        out_specs = (out_specs, pl.BlockSpec(memory_space=pltpu.VMEM))
    return pl.pallas_call(
        functools.partial(_ffn_body, merge=merge is not None, final=g_final is not None, sample=static),
        out_shape=out_shape,
        grid=(n // tm,),
        in_specs=specs,
        out_specs=out_specs,
        scratch_shapes=scratch,
        compiler_params=pltpu.CompilerParams(dimension_semantics=("arbitrary",), vmem_limit_bytes=VMEM_LIMIT),
        name=("ffn_merge" if merge is not None else "ffn") + ("_sample" if sample is not None else ""),
    )(*args)


def _proj_common(x_ref, g_ref, w_ref, bf_ref, tri_ref, width, scale):
    a = width
    h = _rms(x_ref[...], g_ref[...]).astype(BF16)
    p = _dot(h, w_ref[...])
    q = p[:, 0:a] * scale
    k = p[:, a:2 * a]
    v = p[:, 2 * a:3 * a]
    bg = p[:, 3 * a:4 * a]
    uc = p[:, 4 * a:5 * a] * p[:, 5 * a:6 * a]
    z = p[:, 6 * a:6 * a + LANES] + bf_ref[...]
    lane = lax.broadcasted_iota(jnp.int32, z.shape, 1)
    logf = jnp.minimum(z, 0.0) - jnp.log1p(jnp.exp(-jnp.abs(z)))
    l3 = jnp.where(lane < 24, logf, 0.0)
    hi, mid, lo = _split3(l3)
    packed = jnp.where(lane < 8, hi, jnp.where(lane < 16, mid, lo)).astype(BF16)
    r = _dot(tri_ref[...], packed)
    c = r + pltpu.roll(r, LANES - 8, axis=1) + pltpu.roll(r, LANES - 16, axis=1)
    c = jnp.where(lane < 8, c, 0.0)
    return q, k, v, bg, uc, logf, c, lane


def _conv_branch(uc, prev1, prev2, bg, cw_ref, gco_ref):
    y = cw_ref[0:1, :] * prev2 + cw_ref[1:2, :] * prev1 + cw_ref[2:3, :] * uc
    return _rms(bg * y, gco_ref[...]).astype(BF16)


def _proj_prompt_body(x_ref, g_ref, w_ref, bf_ref, tri_ref, cw_ref, gco_ref, psel_ref, asel_ref, hsel_ref,
                      kt_ref, vt_ref, logf_ref, qa_ref, ka_ref, va_ref, cn_ref, tail_ref,
                      carry_c, carry_u, *, n_heads, head_dim, scale):
    tm = x_ref.shape[0]
    width = n_heads * head_dim

    @pl.when(pl.program_id(1) == 0)
    def _():
        carry_c[...] = jnp.zeros_like(carry_c)
        carry_u[...] = jnp.zeros_like(carry_u)

    a = width
    h = _rms(x_ref[...], g_ref[...]).astype(BF16)

    def columns(lo, hi):
        return _dot(h, w_ref[:, lo:hi])

    z = columns(6 * a, 6 * a + LANES) + bf_ref[...]
    p_qk = columns(0, 2 * a)
    lane = lax.broadcasted_iota(jnp.int32, z.shape, 1)
    logf = jnp.minimum(z, 0.0) - jnp.log1p(jnp.exp(-jnp.abs(z)))
    hi, mid, lo = _split3(jnp.where(lane < 24, logf, 0.0))
    packed = jnp.where(lane < 8, hi, jnp.where(lane < 16, mid, lo)).astype(BF16)
    r = _dot(tri_ref[...], packed)
    p_vg = columns(2 * a, 4 * a)
    c = r + pltpu.roll(r, LANES - 8, axis=1) + pltpu.roll(r, LANES - 16, axis=1)
    c = jnp.where(lane < 8, c, 0.0) + carry_c[...]
    carry_c[...] = c[tm - 1:tm, :]
    c2 = c * LOG2E
    logf_ref[0] = logf.T[0:n_heads]
    q = p_qk[:, 0:a] * (scale * LOG2E)
    k = p_qk[:, a:2 * a]

    def packed_parts(x):
        rep = jnp.where(lane < 8, x, jnp.where(lane < 16, pltpu.roll(x, 8, axis=1),
                                               jnp.where(lane < 24, pltpu.roll(x, 16, axis=1), 0.0)))
        hi, mid, lo = _split3(rep)
        return jnp.where(lane < 8, hi, jnp.where(lane < 16, mid, jnp.where(lane < 24, lo,
                         jnp.where(lane == 24, 1.0, 0.0)))).astype(BF16)

    shift = _dot((q * k).astype(BF16), hsel_ref[...])
    k_extra = _dot(packed_parts(c2), psel_ref[...])
    q_extra = _dot_nt(asel_ref[...], packed_parts(c2 - shift))
    p_cu = columns(4 * a, 6 * a)

    q_t = q.T
    kt_ref[0] = k.T
    lane_a = lax.broadcasted_iota(jnp.int32, (tm, AUG), 1)
    pad_q = jnp.zeros((AUG - head_dim - 8, tm), F32)
    for hd in range(n_heads):
        blk = k[:, (hd // 2) * AUG:(hd // 2 + 1) * AUG]
        if hd % 2:
            blk = pltpu.roll(blk, head_dim, axis=1)
        ka_ref[0, hd] = jnp.where(lane_a < head_dim, blk, k_extra[:, hd * AUG:(hd + 1) * AUG]).astype(BF16)
        qa_ref[0, hd] = jnp.concatenate(
            [q_t[hd * head_dim:(hd + 1) * head_dim], q_extra[hd * 8:(hd + 1) * 8], pad_q], axis=0).astype(BF16)

    v_t = p_vg[:, 0:a].T
    bg = p_vg[:, a:2 * a]
    vt_ref[0] = v_t
    row8 = lax.broadcasted_iota(jnp.int32, (8, tm), 0)
    ones_row = jnp.where(row8 == 0, 1.0, 0.0)
    pad_v = jnp.zeros((V_ROWS - head_dim - 8, tm), F32)
    for hd in range(n_heads):
        va_ref[0, hd, 0] = jnp.concatenate(
            [v_t[hd * head_dim:(hd + 1) * head_dim], ones_row, pad_v], axis=0).astype(BF16)

    uc = p_cu[:, 0:a] * p_cu[:, a:2 * a]
    rows = lax.broadcasted_iota(jnp.int32, uc.shape, 0)
    last1 = carry_u[7:8, :]
    last2 = carry_u[6:7, :]
    prev1 = jnp.where(rows == 0, last1, pltpu.roll(uc, 1, axis=0))
    prev2 = jnp.where(rows == 0, last2, jnp.where(rows == 1, last1, pltpu.roll(uc, 2, axis=0)))
    cn_ref[...] = _conv_branch(uc, prev1, prev2, bg, cw_ref, gco_ref)
    tail_ref[0] = uc[tm - 2:tm, :]
    carry_u[...] = uc[tm - 8:tm, :]


def _proj_sample_body(x_ref, g_ref, w_ref, bf_ref, tri_ref, cw_ref, gco_ref, s1_ref, s2_ref,
                      q_ref, k_ref, v_ref, logf_ref, c_ref, uc_ref, cn_ref, *, n_heads, head_dim, scale, seq):
    q, k, v, bg, uc, logf, c, _ = _proj_common(x_ref, g_ref, w_ref, bf_ref, tri_ref, n_heads * head_dim, scale)
    q_ref[...] = q
    k_ref[...] = k
    v_ref[...] = v
    logf_ref[...] = logf[:, 0:n_heads]
    c_ref[...] = c
    uc_ref[...] = uc
    t = lax.broadcasted_iota(jnp.int32, uc.shape, 0) % seq
    prev1 = jnp.where(t == 0, s1_ref[...], pltpu.roll(uc, 1, axis=0))
    prev2 = jnp.where(t < 2, s2_ref[...], pltpu.roll(uc, 2, axis=0))
    cn_ref[...] = _conv_branch(uc, prev1, prev2, bg, cw_ref, gco_ref)


def _selectors(n_heads):
    psel = jnp.zeros((LANES, n_heads * AUG), F32)
    asel = jnp.zeros((n_heads * 8, LANES), F32)
    for h in range(n_heads):
        base = h * AUG + 64
        psel = psel.at[24, base:base + 3].set(1.0)
        for part in range(3):
            psel = psel.at[part * 8 + h, base + 3 + part].set(-1.0)
            asel = asel.at[h * 8 + part, part * 8 + h].set(1.0)
        asel = asel.at[h * 8 + 3:h * 8 + 6, 24].set(1.0)
    return psel.astype(BF16), asel.astype(BF16)


def _proj_prompt(x, g, w, bf, cw, gco, *, batch, n_heads, head_dim):
    n, d = x.shape
    seq = n // batch
    tm = ROW_TILE
    nb = seq // tm
    width = n_heads * head_dim
    tri = jnp.tril(jnp.ones((tm, tm), F32)).astype(BF16)
    psel, asel = _selectors(n_heads)
    hsel = (jnp.arange(width)[:, None] // head_dim == jnp.arange(LANES)[None, :]).astype(BF16)
    row = lambda b, i: (b * nb + i, 0)
    consts = [g, w, bf, tri, cw, gco, psel, asel, hsel]
    out_shape = (
        jax.ShapeDtypeStruct((batch, width, seq), F32),
        jax.ShapeDtypeStruct((batch, width, seq), F32),
        jax.ShapeDtypeStruct((batch, n_heads, seq), F32),
        jax.ShapeDtypeStruct((batch, n_heads, AUG, seq), BF16),
        jax.ShapeDtypeStruct((batch, n_heads, seq, AUG), BF16),
        jax.ShapeDtypeStruct((batch, n_heads, nb, V_ROWS, tm), BF16),
        jax.ShapeDtypeStruct((n, width), BF16),
        jax.ShapeDtypeStruct((batch, 2, width), F32),
    )
    out_specs = (
        pl.BlockSpec((1, width, tm), lambda b, i: (b, 0, i)),
        pl.BlockSpec((1, width, tm), lambda b, i: (b, 0, i)),
        pl.BlockSpec((1, n_heads, tm), lambda b, i: (b, 0, i)),
        pl.BlockSpec((1, n_heads, AUG, tm), lambda b, i: (b, 0, 0, i)),
        pl.BlockSpec((1, n_heads, tm, AUG), lambda b, i: (b, 0, i, 0)),
        pl.BlockSpec((1, n_heads, 1, V_ROWS, tm), lambda b, i: (b, 0, i, 0, 0)),
        pl.BlockSpec((tm, width), row),
        pl.BlockSpec((1, 2, width), lambda b, i: (b, 0, 0)),
    )
    return pl.pallas_call(
        functools.partial(_proj_prompt_body, n_heads=n_heads, head_dim=head_dim, scale=1.0 / math.sqrt(head_dim)),
        out_shape=out_shape,
        grid=(batch, nb),
        in_specs=[pl.BlockSpec((tm, d), row)] + [_const_spec(a) for a in consts],
        out_specs=out_specs,
        scratch_shapes=[pltpu.VMEM((1, LANES), F32), pltpu.VMEM((8, width), F32)],
        compiler_params=pltpu.CompilerParams(dimension_semantics=("arbitrary", "arbitrary"),
                                             vmem_limit_bytes=VMEM_LIMIT),
        name="proj_prompt",
    )(x, *consts)


def _proj_sample(x, g, w, bf, cw, gco, s1, s2, *, seq, n_heads, head_dim):
    n, d = x.shape
    width = n_heads * head_dim
    r = jnp.arange(n)
    tri = ((r[None, :] <= r[:, None]) & (r[None, :] // seq == r[:, None] // seq)).astype(BF16)
    args = [x, g, w, bf, tri, cw, gco, s1, s2]
    shapes = [(n, width)] * 3 + [(n, n_heads), (n, LANES), (n, width)]
    out_shape = tuple(jax.ShapeDtypeStruct(s, F32) for s in shapes) + (jax.ShapeDtypeStruct((n, width), BF16),)
    return pl.pallas_call(
        functools.partial(_proj_sample_body, n_heads=n_heads, head_dim=head_dim,
                          scale=1.0 / math.sqrt(head_dim), seq=seq),
        out_shape=out_shape,
        compiler_params=pltpu.CompilerParams(vmem_limit_bytes=VMEM_LIMIT),
        name="proj_sample",
    )(*args)


def _attn_prompt_body(qa_ref, ka_ref, va_ref, o_ref, acc_ref, top_ref, s_ref, ot_ref, *, head_dim):
    i = pl.program_id(2)
    n_h = qa_ref.shape[1]
    tq = qa_ref.shape[3]
    tk = tq
    sub = lax.broadcasted_iota(jnp.int32, (tk, tq), 0)
    lan = lax.broadcasted_iota(jnp.int32, (tk, tq), 1)
    causal = sub <= lan

    def scores(j, hh):
        k_j = ka_ref[0, hh, pl.ds(pl.multiple_of(j * tk, tk), tk), :]
        return _dot(k_j, qa_ref[0, hh])

    def normalised(acc):
        return acc[0:head_dim] / acc[head_dim:head_dim + 1]

    def block_units(j, j_next, s_first, masked):
        s_next = s_first
        for hh in range(n_h):
            s = jnp.where(causal, s_next, NEG) if masked else s_next
            s_next = scores(j, hh + 1) if hh + 1 < n_h else scores(j_next, 0)
            blk_top = jnp.max(s, axis=0, keepdims=True)
            pv = _dot(va_ref[0, hh, j], jnp.exp2(s).astype(BF16))
            if masked:
                top_ref[hh], acc_ref[hh] = blk_top, pv
            else:
                top_ref[hh] = jnp.maximum(top_ref[hh], blk_top)
                acc_ref[hh] += pv
        return s_next

    s_ref[...] = block_units(i, 0, scores(i, 0), True)

    def block_pair(p, _):
        s_mid = block_units(2 * p, 2 * p + 1, s_ref[...], False)
        s_ref[...] = block_units(2 * p + 1, 2 * p + 2, s_mid, False)
        return 0

    lax.fori_loop(0, i // 2, block_pair, 0)

    @pl.when(i % 2 == 1)
    def _():
        block_units(i - 1, i, s_ref[...], False)

    o_ref[0] = jnp.concatenate([normalised(acc_ref[hh]) for hh in range(n_h)], axis=0).T
    hi, lo = top_ref[0], top_ref[0]
    for hh in range(1, n_h):
        hi, lo = jnp.maximum(hi, top_ref[hh]), jnp.minimum(lo, top_ref[hh])

    @pl.when((jnp.max(hi) > SHIFT_SLACK) | (jnp.min(lo) < -SHIFT_SLACK))
    def _():
        def online_head(hh, _):
            q_t = qa_ref[0, hh]

            def sc(j):
                return _dot(ka_ref[0, hh, pl.ds(pl.multiple_of(j * tk, tk), tk), :], q_t)

            s = jnp.where(causal, sc(i), NEG)
            m = jnp.max(s, axis=0, keepdims=True)
            acc = _dot(va_ref[0, hh, i], jnp.exp2(s - m).astype(BF16))

            def body(j, carry):
                m, acc = carry
                s = sc(j)
                m_new = jnp.maximum(m, jnp.max(s, axis=0, keepdims=True))
                p = jnp.exp2(s - m_new).astype(BF16)
                return m_new, jnp.exp2(m - m_new) * acc + _dot(va_ref[0, hh, j], p)

            m, acc = lax.fori_loop(0, i, body, (m, acc))
            ot_ref[pl.ds(pl.multiple_of(hh * head_dim, head_dim), head_dim), :] = normalised(acc)
            return 0

        lax.fori_loop(0, n_h, online_head, 0)
        o_ref[0] = ot_ref[...].T


def _attn_prompt(qa, ka, va, *, head_dim):
    batch, n_heads, _, seq = qa.shape
    nb, tq = va.shape[2], va.shape[4]
    hps = HEADS_PER_STEP
    resident = dict(pipeline_mode=pl.Buffered(1))
    return pl.pallas_call(
        functools.partial(_attn_prompt_body, head_dim=head_dim),
        out_shape=jax.ShapeDtypeStruct((batch, seq, n_heads * head_dim), F32),
        grid=(batch, n_heads // hps, nb),
        in_specs=[
            pl.BlockSpec((1, hps, AUG, tq), lambda b, g, i: (b, g, 0, i)),
            pl.BlockSpec((1, hps, seq, AUG), lambda b, g, i: (b, g, 0, 0), **resident),
            pl.BlockSpec((1, hps, nb, V_ROWS, tq), lambda b, g, i: (b, g, 0, 0, 0), **resident),
        ],
        out_specs=pl.BlockSpec((1, tq, hps * head_dim), lambda b, g, i: (b, i, g)),
        scratch_shapes=[pltpu.VMEM((hps, V_ROWS, tq), F32), pltpu.VMEM((hps, 1, tq), F32), pltpu.VMEM((tq, tq), F32),
                        pltpu.VMEM((hps * head_dim, tq), F32)],
        compiler_params=pltpu.CompilerParams(dimension_semantics=("arbitrary",) * 3, vmem_limit_bytes=VMEM_LIMIT),
        name="attn_prompt",
    )(qa, ka, va)


def _suffix_body(x_ref, rhs_ref, o_ref):
    hi, mid, lo = _split3(x_ref[...])
    rhs = rhs_ref[...]
    o_ref[...] = _dot(hi.astype(BF16), rhs) + _dot(mid.astype(BF16), rhs) + _dot(lo.astype(BF16), rhs)


def _suffix_pool(lf_pool):
    n_pool, n_heads, page = lf_pool.shape
    rows = n_pool * n_heads
    tr = SUFFIX_ROWS
    assert rows % tr == 0
    r = jnp.arange(page)
    rhs = jnp.concatenate([(r[:, None] > r[None, :]).astype(BF16), jnp.ones((page, page), BF16)], axis=1)
    out = pl.pallas_call(
        _suffix_body,
        out_shape=jax.ShapeDtypeStruct((rows, 2 * page), F32),
        grid=(rows // tr,),
        in_specs=[pl.BlockSpec((tr, page), lambda i: (i, 0)), _const_spec(rhs)],
        out_specs=pl.BlockSpec((tr, 2 * page), lambda i: (i, 0)),
        compiler_params=pltpu.CompilerParams(dimension_semantics=("arbitrary",), vmem_limit_bytes=VMEM_LIMIT),
        name="suffix_pool",
    )(lf_pool.reshape(rows, page), rhs)
    return out.reshape(n_pool, n_heads, 2 * page)


_N_SAMPLE_IN = 9


def _sample_chunk_pair(t, pt_ref, qbd_ref, cq_ref, knew_ref, vnew_ref, cnew_ref, kt_hbm, vt_hbm, sfx_hbm, o_ref,
                       kbuf, vbuf, sbuf, sem, m_ref, l_ref, lsum_ref, top_ref, acc_ref, carry_ref,
                       *, g, n_pages, seq, n_heads, head_dim):
    dec_batch = qbd_ref.shape[0]
    n_new = knew_ref.shape[1]
    per_row = n_pages // g
    n_chunks = dec_batch * per_row

    def first_page(c):
        return (c // per_row) * n_pages + (n_pages - 1) - (c % per_row) * g

    def page_copies(first, slot, p_i):
        page = pt_ref[first - p_i]
        return (pltpu.make_async_copy(kt_hbm.at[page], kbuf.at[slot, p_i], sem.at[0, slot]),
                pltpu.make_async_copy(vt_hbm.at[page], vbuf.at[slot, p_i], sem.at[1, slot]),
                pltpu.make_async_copy(sfx_hbm.at[page], sbuf.at[slot, p_i], sem.at[2, slot]))

    def start_chunk(c, slot):
        first = first_page(c)

        def body(p_i, _):
            for cp in page_copies(first, slot, p_i):
                cp.start()
            return 0
        lax.fori_loop(0, g, body, 0, unroll=COPY_UNROLL)

    def wait_chunk(c, slot):
        first = first_page(c)

        def body(p_i, _):
            for cp in page_copies(first, slot, p_i):
                cp.wait()
            return 0
        lax.fori_loop(0, g, body, 0, unroll=COPY_UNROLL)

    def start_row(b):
        qb, cq = qbd_ref[b], cq_ref[b]
        s = _dot_nt(qb, knew_ref[b]) + cq[:, 0:n_new] - cnew_ref[b]
        t_row = lax.broadcasted_iota(jnp.int32, s.shape, 0) // n_heads
        j = lax.broadcasted_iota(jnp.int32, s.shape, 1)
        s = jnp.where((j <= t_row) & (j < seq), s, NEG)
        m = jnp.max(s, axis=1, keepdims=True)
        p = jnp.exp(s - m)
        m_ref[...] = m
        l_ref[...] = jnp.sum(p, axis=1, keepdims=True)
        acc_ref[...] = _dot(p.astype(BF16), vnew_ref[b])
        lsum_ref[...] = jnp.zeros_like(lsum_ref)
        top_ref[...] = jnp.full_like(top_ref, NEG)
        carry_ref[...] = jnp.zeros_like(carry_ref)

    def page_scores(b, slot):
        qb, cq = qbd_ref[b], cq_ref[b]
        carry = carry_ref[...]
        parts = []
        for p_i in range(g):
            sfx = sbuf[slot, p_i]
            suffix = sfx[:, 0:LANES] + carry
            carry = carry + sfx[:, LANES:2 * LANES]
            bias = jnp.concatenate([suffix] * seq, axis=0) + cq
            parts.append(_dot(qb, kbuf[slot, p_i].astype(BF16)) + bias)
        carry_ref[...] = carry
        return parts

    def page_values(slot, p_i):
        return vbuf[slot, p_i].T.astype(BF16)

    def one_pass_chunk(b, slot):
        m = m_ref[...]
        top, lsum, acc = top_ref[...], lsum_ref[...], acc_ref[...]
        for p_i, s in enumerate(page_scores(b, slot)):
            top = jnp.maximum(top, s)
            p = jnp.exp(s - m)
            lsum = lsum + p
            acc = acc + _dot(p.astype(BF16), page_values(slot, p_i))
        top_ref[...], lsum_ref[...], acc_ref[...] = top, lsum, acc

    def online_chunk(b, slot):
        s = jnp.concatenate(page_scores(b, slot), axis=1)
        m_prev = m_ref[...]
        m_new = jnp.maximum(m_prev, jnp.max(s, axis=1, keepdims=True))
        p = jnp.exp(s - m_new)
        alpha = jnp.exp(m_prev - m_new)
        l_ref[...] = alpha * l_ref[...] + jnp.sum(p, axis=1, keepdims=True)
        pb = p.astype(BF16)
        acc = alpha * acc_ref[...]
        for p_i in range(g):
            acc = acc + _dot(pb[:, p_i * LANES:(p_i + 1) * LANES], page_values(slot, p_i))
        acc_ref[...] = acc
        m_ref[...] = m_new

    def write_row(b, l):
        o = acc_ref[...] / l
        r = lax.broadcasted_iota(jnp.int32, o.shape, 0)
        col = lax.broadcasted_iota(jnp.int32, o.shape, 1)
        o = jnp.where(col // head_dim == r % n_heads, o, 0.0)
        o_ref[b] = jnp.sum(o.reshape(seq, n_heads, o.shape[1]), axis=1)

    def finish_row(b, slot):
        write_row(b, l_ref[...] + jnp.sum(lsum_ref[...], axis=1, keepdims=True))
        excess = jnp.max(top_ref[...], axis=1, keepdims=True) - m_ref[...]

        @pl.when(jnp.max(excess) > SAMPLE_SLACK)
        def _():
            start_row(b)

            def redo(r, _):
                start_chunk(b * per_row + r, slot)
                wait_chunk(b * per_row + r, slot)
                online_chunk(b, slot)
                return 0

            lax.fori_loop(0, per_row, redo, 0)
            write_row(b, l_ref[...])

    @pl.when(t == 0)
    def _():
        start_chunk(0, 0)
        start_chunk(1, 1)

    for slot in (0, 1):
        c = 2 * t + slot
        b, r = c // per_row, c % per_row

        @pl.when(r == 0)
        def _():
            start_row(b)

        wait_chunk(c, slot)
        one_pass_chunk(b, slot)

        if slot == 1:
            @pl.when(r == per_row - 1)
            def _():
                finish_row(b, slot)

        @pl.when(c + 2 < n_chunks)
        def _():
            start_chunk(c + 2, slot)


def _sample_specs(pt_flat, qbd, cq, knew, vnew, cnew, kt_pool, vt_pool, sfx_pool, *, g, n_pages, seq, n_heads,
                  head_dim):
    dec_batch, rows, width = qbd.shape
    page = kt_pool.shape[2]
    assert n_pages % (2 * g) == 0 and pt_flat.shape[0] == dec_batch * n_pages
    vmem = pl.BlockSpec(memory_space=pltpu.VMEM)
    hbm = pl.BlockSpec(memory_space=pl.ANY)
    in_specs = [pl.BlockSpec(memory_space=pltpu.SMEM)] + [vmem] * 5 + [hbm] * 3
    scratch = [
        pltpu.VMEM((2, g, width, page), F32), pltpu.VMEM((2, g, width, page), F32),
        pltpu.VMEM((2, g, n_heads, 2 * page), F32), pltpu.SemaphoreType.DMA((3, 2)),
        pltpu.VMEM((rows, 1), F32), pltpu.VMEM((rows, 1), F32), pltpu.VMEM((rows, LANES), F32),
        pltpu.VMEM((rows, LANES), F32), pltpu.VMEM((rows, width), F32), pltpu.VMEM((n_heads, LANES), F32),
    ]
    return in_specs, jax.ShapeDtypeStruct((dec_batch, seq, width), F32), scratch, dec_batch * n_pages // (2 * g)


def kernel(x_prompt, x_sample, cache_k, cache_v, cache_logf, state_conv, page_table, g_ffn1, w1_ffn1, w3_ffn1,
           w2_ffn1, g_mix, w_in, b_f, conv_w, g_attn_out, g_conv_out, w_out, g_ffn2, w1_ffn2, w3_ffn2, w2_ffn2,
           g_final):
    batch, seq, d_model = x_prompt.shape
    dec_batch, dec_seq, _ = x_sample.shape
    depth, n_pool, page, n_heads, head_dim = cache_k.shape
    width = n_heads * head_dim
    assert depth == 1 and seq % ROW_TILE == 0 and page == LANES and 2 * head_dim == LANES
    assert w_in.shape[2] == 6 * width + n_heads

    xp = x_prompt.reshape(batch * seq, d_model)
    xs = x_sample.reshape(dec_batch * dec_seq, d_model)
    row = lambda a: a.reshape(1, -1)
    l = 0

    bw = lambda a: a.astype(BF16)
    wi = w_in[l]
    wf = jnp.tile(wi[:, 3 * width:3 * width + n_heads], (1, 3))
    w_proj = bw(jnp.concatenate([wi[:, :3 * width], wi[:, 3 * width + n_heads:],
                                 jnp.pad(wf, ((0, 0), (0, LANES - 3 * n_heads)))], axis=1))
    bf = jnp.pad(jnp.tile(b_f[l], 3), (0, LANES - 3 * n_heads)).reshape(1, LANES)
    ffn1 = (row(g_ffn1[l]), bw(w1_ffn1[l]), bw(w3_ffn1[l]), bw(w2_ffn1[l]))
    ffn2 = (row(g_ffn2[l]), bw(w1_ffn2[l]), bw(w3_ffn2[l]), bw(w2_ffn2[l]))
    wo = bw(w_out[l])
    merge_w = (row(g_attn_out[l]), wo[:width], wo[width:])
    gmix, cw, gco, gfin = row(g_mix[l]), conv_w[l], row(g_conv_out[l]), row(g_final)

    xs1 = _ffn(xs, *ffn1)
    st = state_conv[l]
    s1 = jnp.repeat(st[:, 1:2], dec_seq, axis=1).reshape(dec_batch * dec_seq, width)
    s2 = jnp.pad(st, ((0, 0), (0, dec_seq - 2), (0, 0))).reshape(dec_batch * dec_seq, width)
    q_s, k_s, v_s, logf_s, c_s, uc_s, cn_s = _proj_sample(
        xs1, gmix, w_proj, bf, cw, gco, s1, s2, seq=dec_seq, n_heads=n_heads, head_dim=head_dim)

    eye = jnp.eye(n_heads, dtype=F32)
    q4 = q_s.reshape(dec_batch, dec_seq, n_heads, 1, head_dim)
    qbd = bw((q4 * eye[None, None, :, :, None]).reshape(dec_batch, dec_seq * n_heads, width))
    c4 = c_s[:, :n_heads].reshape(dec_batch, dec_seq, n_heads)
    cq = jnp.broadcast_to(c4.reshape(dec_batch, dec_seq * n_heads, 1), (dec_batch, dec_seq * n_heads, LANES))
    n_new = 16
    cnew = jnp.broadcast_to(jnp.transpose(c4, (0, 2, 1))[:, None], (dec_batch, dec_seq, n_heads, dec_seq))
    cnew = jnp.pad(cnew.reshape(dec_batch, dec_seq * n_heads, dec_seq), ((0, 0), (0, 0), (0, n_new - dec_seq)))
    pad_new = lambda a: bw(jnp.pad(a.reshape(dec_batch, dec_seq, width), ((0, 0), (0, n_new - dec_seq), (0, 0))))
    kt_pool = jnp.transpose(cache_k[l], (0, 2, 3, 1)).reshape(n_pool, width, page)
    vt_pool = jnp.transpose(cache_v[l], (0, 2, 3, 1)).reshape(n_pool, width, page)
    sfx_pool = _suffix_pool(jnp.transpose(cache_logf[l], (0, 2, 1)))
    n_pages = page_table.shape[1]
    half = dec_batch // 2
    static = dict(g=PAGES_PER_STEP, n_pages=n_pages, seq=dec_seq, n_heads=n_heads, head_dim=head_dim)

    def sample_part(lo):
        rows = slice(lo, lo + half)
        per_row = (page_table[rows].reshape(-1), qbd[rows], cq[rows], pad_new(k_s)[rows], pad_new(v_s)[rows],
                   cnew[rows])
        return per_row + (kt_pool, vt_pool, sfx_pool), static

    xp1, o_s_lo = _ffn(xp, *ffn1, sample=sample_part(0))
    kt_p, vt_p, logft_p, qa, ka, va, cn_p, tail_p = _proj_prompt(
        xp1, gmix, w_proj, bf, cw, gco, batch=batch, n_heads=n_heads, head_dim=head_dim)
    o_p = _attn_prompt(qa, ka, va, head_dim=head_dim).reshape(batch * seq, width)
    y_p, o_s_hi = _ffn(xp1, *ffn2, merge=(o_p, cn_p) + merge_w, g_final=gfin, sample=sample_part(half))

    o_s = jnp.concatenate([o_s_lo, o_s_hi], axis=0).reshape(dec_batch * dec_seq, width)
    y_s = _ffn(xs1, *ffn2, merge=(o_s, cn_s) + merge_w, g_final=gfin)

    hd = (n_heads, head_dim)
    untranspose = lambda a: jnp.transpose(a.reshape(batch, n_heads, head_dim, seq), (0, 3, 1, 2))[None]
    return (
        y_p.reshape(batch, seq, d_model),
        y_s.reshape(dec_batch, dec_seq, d_model),
        untranspose(kt_p),
        untranspose(vt_p),
        jnp.transpose(logft_p, (0, 2, 1))[None],
        tail_p.reshape(1, batch, 2, width),
        k_s.reshape(1, dec_batch, dec_seq, *hd),
        v_s.reshape(1, dec_batch, dec_seq, *hd),
        logf_s.reshape(1, dec_batch, dec_seq, n_heads),
        uc_s.reshape(1, dec_batch, dec_seq, width)[:, :, dec_seq - 2:],
    )
```
